```python
import math
import jax, jax.numpy as jnp
from jax import lax
import numpy as np

D_MODEL = 2048
BATCH = 2
SEQ = 4096
DEPTH = 4
DEC_BATCH = 32
DEC_SEQ = 4
PAST_LEN = 16384
PAGE_SIZE = 128

N_A_LAYERS = DEPTH // 2
N_B_LAYERS = DEPTH - N_A_LAYERS
A_HEADS = 8
A_KV_HEADS = 4
A_GROUP = A_HEADS // A_KV_HEADS
A_HEAD_DIM = D_MODEL // A_HEADS // 2
A_V_DIM = 2 * A_HEAD_DIM
A_Q_W = A_HEADS * 2 * A_HEAD_DIM
A_K_W = A_KV_HEADS * 2 * A_HEAD_DIM
A_V_W = A_KV_HEADS * A_V_DIM
Q_BLOCK = 128
B_HEAD_DIM = 64
B_HEADS = D_MODEL // B_HEAD_DIM
B_KV_HEADS = 8
B_GROUP = B_HEADS // B_KV_HEADS
WINDOW = 128
D_FF = 5632
CONV_W = 3
EPS = 1e-6
NEG = -1e30

kernel_name = "yoco_diffattn_swa_sink_convffn_step"


def alibi_slopes(n):
    return jnp.asarray(2.0 ** (-8.0 * np.arange(1, n + 1, dtype=np.float32) / n), jnp.float32)


def rmsnorm(x, g):
    x32 = x.astype(jnp.float32)
    y = x32 * lax.rsqrt(jnp.mean(x32 * x32, axis=-1, keepdims=True) + EPS)
    return (y * g.astype(jnp.float32)).astype(x.dtype)


def modulate(x, g, shift, scale):
    return rmsnorm(x, g) * (1 + scale[:, None, :]) + shift[:, None, :]


def diff_attn(q, qpos, segments, lam, slopes):
    scale = A_HEAD_DIM ** -0.5
    scores = []
    for k, v, kpos in segments:
        s = jnp.einsum('bqngid,bknid->bngiqk', q, k).astype(jnp.float32) * scale
        dist = (qpos[:, None] - kpos[None, :]).astype(jnp.float32)
        s = jnp.where(dist >= 0, s - slopes[None, :, :, None, None, None] * dist, NEG)
        scores.append(s)
    p = jax.nn.softmax(jnp.concatenate(scores, axis=-1), axis=-1)
    a = p[:, :, :, 0] - lam * p[:, :, :, 1]
    outs = []
    off = 0
    for k, v, kpos in segments:
        n = kpos.shape[0]
        outs.append(jnp.einsum('bngqk,bknv->bqngv', a[..., off:off + n].astype(v.dtype), v))
        off += n
    out = outs[0]
    for o in outs[1:]:
        out = out + o
    return out


def mixer_a(h, w_qkv, lam_p, subln_g, w_o, layer_idx, pos0, past):
    B, T, _ = h.shape
    qkv = h @ w_qkv
    q, k, v = jnp.split(qkv, [A_Q_W, A_Q_W + A_K_W], axis=-1)
    q = q.reshape(B, T, A_KV_HEADS, A_GROUP, 2, A_HEAD_DIM)
    k = k.reshape(B, T, A_KV_HEADS, 2, A_HEAD_DIM)
    v = v.reshape(B, T, A_KV_HEADS, A_V_DIM)
    lam_init = 0.8 - 0.6 * math.exp(-0.3 * layer_idx)
    lp = lam_p.astype(jnp.float32)
    lam = jnp.exp(jnp.sum(lp[0] * lp[1])) - jnp.exp(jnp.sum(lp[2] * lp[3])) + lam_init
    slopes = alibi_slopes(A_HEADS).reshape(A_KV_HEADS, A_GROUP)
    pos = pos0 + jnp.arange(T, dtype=jnp.int32)
    if past is None:
        nb = T // Q_BLOCK
        qb = jnp.moveaxis(q.reshape(B, nb, Q_BLOCK, A_KV_HEADS, A_GROUP, 2, A_HEAD_DIM), 1, 0)
        pb = pos.reshape(nb, Q_BLOCK)
        o = lax.map(lambda a: diff_attn(a[0], a[1], [(k, v, pos)], lam, slopes), (qb, pb))
        o = jnp.moveaxis(o, 0, 1).reshape(B, T, A_HEADS, A_V_DIM)
    else:
        pk, pv = past
        ppos = jnp.arange(pk.shape[1], dtype=jnp.int32)
        o = diff_attn(q, pos, [(pk, pv, ppos), (k, v, pos)], lam, slopes).reshape(B, T, A_HEADS, A_V_DIM)
    o = rmsnorm(o, subln_g) * (1 - lam_init)
    return o.reshape(B, T, A_HEADS * A_V_DIM) @ w_o, k, v


def swa_attn(q, qpos, k, v, kpos, sinks, slopes):
    s = jnp.einsum('bqngd,bknd->bngqk', q, k).astype(jnp.float32) * (B_HEAD_DIM ** -0.5)
    dist = qpos[:, None] - kpos[None, :]
    valid = (dist >= 0) & (dist < WINDOW) & (kpos[None, :] >= 0)
    s = jnp.where(valid, s - slopes[None, :, :, None, None] * dist.astype(jnp.float32), NEG)
    sink = jnp.broadcast_to(sinks.astype(jnp.float32)[None, :, :, None, None], s.shape[:-1] + (1,))
    p = jax.nn.softmax(jnp.concatenate([s, sink], axis=-1), axis=-1)[..., :-1]
    return jnp.einsum('bngqk,bknd->bqngd', p.astype(v.dtype), v)


def mixer_b(h, w_q, sinks, w_o, kv, pos0, past):
    B, T, _ = h.shape
    q = (h @ w_q).reshape(B, T, B_KV_HEADS, B_GROUP, B_HEAD_DIM)
    k, v = kv
    slopes = alibi_slopes(B_HEADS).reshape(B_KV_HEADS, B_GROUP)
    sk = sinks.reshape(B_KV_HEADS, B_GROUP)
    pos = pos0 + jnp.arange(T, dtype=jnp.int32)
    if past is None:
        nb = T // WINDOW
        qb = q.reshape(B, nb, WINDOW, B_KV_HEADS, B_GROUP, B_HEAD_DIM)
        kb = k.reshape(B, nb, WINDOW, B_KV_HEADS, B_HEAD_DIM)
        vb = v.reshape(B, nb, WINDOW, B_KV_HEADS, B_HEAD_DIM)
        kk = jnp.concatenate([jnp.concatenate([jnp.zeros_like(kb[:, :1]), kb[:, :-1]], axis=1), kb], axis=2)
        vv = jnp.concatenate([jnp.concatenate([jnp.zeros_like(vb[:, :1]), vb[:, :-1]], axis=1), vb], axis=2)
        qpos = pos.reshape(nb, WINDOW)
        kpos = (jnp.arange(nb, dtype=jnp.int32) * WINDOW)[:, None] - WINDOW + jnp.arange(2 * WINDOW, dtype=jnp.int32)[None, :]
        o = jax.vmap(swa_attn, in_axes=(1, 0, 1, 1, 0, None, None), out_axes=1)(qb, qpos, kk, vv, kpos, sk, slopes)
    else:
        pk, pv = past
        kpos = pos0 - pk.shape[1] + jnp.arange(pk.shape[1] + T, dtype=jnp.int32)
        o = swa_attn(q, pos, jnp.concatenate([pk, k], axis=1), jnp.concatenate([pv, v], axis=1), kpos, sk, slopes)
    return o.reshape(B, T, B_HEADS * B_HEAD_DIM) @ w_o


def shared_kv(x, c, kv_norm_g, kv_ada_w, kv_ada_b, w_kv):
    B, T, _ = x.shape
    shift, scale = jnp.split(jax.nn.silu(c) @ kv_ada_w + kv_ada_b, 2, axis=-1)
    kv = modulate(x, kv_norm_g, shift, scale) @ w_kv
    k, v = jnp.split(kv, 2, axis=-1)
    return k.reshape(B, T, B_KV_HEADS, B_HEAD_DIM), v.reshape(B, T, B_KV_HEADS, B_HEAD_DIM)


def conv_ffn(h, conv_prev, w_in, conv_w, conv_b, w_out):
    T = h.shape[1]
    u = h @ w_in
    ext = jnp.concatenate([conv_prev.astype(u.dtype), u], axis=1)
    uc = conv_b
    for j in range(CONV_W):
        uc = uc + ext[:, j:j + T] * conv_w[j]
    gate, val = jnp.split(uc, 2, axis=-1)
    y = (jax.nn.gelu(gate, approximate=True) * val) @ w_out
    return y, ext[:, -(CONV_W - 1):]


def trunk(x, c, pos0, cache_a_k, cache_a_v, page_table, state_b_k, state_b_v, state_conv,
          ada_w, ada_b, norm_g, w_qkv_a, lambda_a, subln_g, w_o_a, kv_norm_g, kv_ada_w, kv_ada_b,
          w_kv_b, w_q_b, sinks_b, w_o_b, w_ffn_in, conv_w, conv_b, w_ffn_out):
    B, T, _ = x.shape
    a_k_rows, a_v_rows, conv_rows = [], [], []
    kv_b, b_past, new_b = None, None, None
    for l in range(DEPTH):
        mod = jax.nn.silu(c) @ ada_w[l] + ada_b[l]
        sh1, sc1, g1, sh2, sc2, g2 = jnp.split(mod, 6, axis=-1)
        h = modulate(x, norm_g[l, 0], sh1, sc1)
        if l < N_A_LAYERS:
            past = None
            if cache_a_k is not None:
                past = (cache_a_k[l, page_table].reshape(B, -1, A_KV_HEADS, 2, A_HEAD_DIM),
                        cache_a_v[l, page_table].reshape(B, -1, A_KV_HEADS, A_V_DIM))
            m, k_new, v_new = mixer_a(h, w_qkv_a[l], lambda_a[l], subln_g[l], w_o_a[l], l, pos0, past)
            a_k_rows.append(k_new)
            a_v_rows.append(v_new)
        else:
            j = l - N_A_LAYERS
            m = mixer_b(h, w_q_b[j], sinks_b[j], w_o_b[j], kv_b, pos0, b_past)
        x = x + g1[:, None, :] * rmsnorm(m, norm_g[l, 1])
        h2 = modulate(x, norm_g[l, 2], sh2, sc2)
        if state_conv is None:
            prev = jnp.zeros((B, CONV_W - 1, 2 * D_FF), h2.dtype)
        else:
            prev = state_conv[l]
        f, conv_new = conv_ffn(h2, prev, w_ffn_in[l], conv_w[l], conv_b[l], w_ffn_out[l])
        conv_rows.append(conv_new)
        x = x + g2[:, None, :] * rmsnorm(f, norm_g[l, 3])
        if l == N_A_LAYERS - 1:
            kb, vb = shared_kv(x, c, kv_norm_g, kv_ada_w, kv_ada_b, w_kv_b)
            kv_b = (kb, vb)
            if state_b_k is None:
                new_b = (kb[:, -WINDOW:], vb[:, -WINDOW:])
            else:
                b_past = (state_b_k.astype(kb.dtype), state_b_v.astype(vb.dtype))
                new_b = (jnp.concatenate([b_past[0], kb], axis=1)[:, -WINDOW:],
                         jnp.concatenate([b_past[1], vb], axis=1)[:, -WINDOW:])
    return x, jnp.stack(a_k_rows), jnp.stack(a_v_rows), new_b[0], new_b[1], jnp.stack(conv_rows)


def setup_inputs(seed: int = 0) -> dict:
    key = jax.random.key(seed)
    ks = jax.random.split(key, 32)
    n_pages = PAST_LEN // PAGE_SIZE
    n_used = DEC_BATCH * n_pages
    n_pool = n_used + max(1, n_used // 4)

    def nrm(k, shape, s=1.0):
        return jax.random.normal(k, shape, jnp.float32) * s

    d = D_MODEL
    page_table = jax.random.permutation(ks[0], n_pool)[:n_used].reshape(DEC_BATCH, n_pages).astype(jnp.int32)
    return {
        "x_prompt": nrm(ks[1], (BATCH, SEQ, d)),
        "x_sample": nrm(ks[2], (DEC_BATCH, DEC_SEQ, d)),
        "c_prompt": nrm(ks[3], (BATCH, d)),
        "c_sample": nrm(ks[4], (DEC_BATCH, d)),
        "cache_a_k": nrm(ks[5], (N_A_LAYERS, n_pool, PAGE_SIZE, A_KV_HEADS, 2, A_HEAD_DIM)),
        "cache_a_v": nrm(ks[6], (N_A_LAYERS, n_pool, PAGE_SIZE, A_KV_HEADS, A_V_DIM)),
        "page_table": page_table,
        "state_b_k": nrm(ks[7], (DEC_BATCH, WINDOW, B_KV_HEADS, B_HEAD_DIM)),
        "state_b_v": nrm(ks[8], (DEC_BATCH, WINDOW, B_KV_HEADS, B_HEAD_DIM)),
        "state_conv": nrm(ks[9], (DEPTH, DEC_BATCH, CONV_W - 1, 2 * D_FF)),
        "ada_w": nrm(ks[10], (DEPTH, d, 6 * d), 0.5 * d ** -0.5),
        "ada_b": nrm(ks[11], (DEPTH, 6 * d), 0.01),
        "norm_g": 1.0 + nrm(ks[12], (DEPTH, 4, d), 0.01),
        "w_qkv_a": nrm(ks[13], (N_A_LAYERS, d, A_Q_W + A_K_W + A_V_W), d ** -0.5),
        "lambda_a": nrm(ks[14], (N_A_LAYERS, 4, A_HEAD_DIM), 0.1),
        "subln_g": 1.0 + nrm(ks[15], (N_A_LAYERS, A_V_DIM), 0.01),
        "w_o_a": nrm(ks[16], (N_A_LAYERS, A_HEADS * A_V_DIM, d), (A_HEADS * A_V_DIM) ** -0.5),
        "kv_norm_g": 1.0 + nrm(ks[17], (d,), 0.01),
        "kv_ada_w": nrm(ks[18], (d, 2 * d), 0.5 * d ** -0.5),
        "kv_ada_b": nrm(ks[19], (2 * d,), 0.01),
        "w_kv_b": nrm(ks[20], (d, 2 * B_KV_HEADS * B_HEAD_DIM), d ** -0.5),
        "w_q_b": nrm(ks[21], (N_B_LAYERS, d, B_HEADS * B_HEAD_DIM), d ** -0.5),
        "sinks_b": nrm(ks[22], (N_B_LAYERS, B_HEADS), 0.5),
        "w_o_b": nrm(ks[23], (N_B_LAYERS, B_HEADS * B_HEAD_DIM, d), (B_HEADS * B_HEAD_DIM) ** -0.5),
        "w_ffn_in": nrm(ks[24], (DEPTH, d, 2 * D_FF), d ** -0.5),
        "conv_w": nrm(ks[25], (DEPTH, CONV_W, 2 * D_FF), CONV_W ** -0.5),
        "conv_b": nrm(ks[26], (DEPTH, 2 * D_FF), 0.01),
        "w_ffn_out": nrm(ks[27], (DEPTH, D_FF, d), D_FF ** -0.5),
    }


def reference(x_prompt, x_sample, c_prompt, c_sample, cache_a_k, cache_a_v, page_table, state_b_k, state_b_v,
              state_conv, ada_w, ada_b, norm_g, w_qkv_a, lambda_a, subln_g, w_o_a, kv_norm_g, kv_ada_w, kv_ada_b,
              w_kv_b, w_q_b, sinks_b, w_o_b, w_ffn_in, conv_w, conv_b, w_ffn_out):
    y_prompt, a_k_prompt, a_v_prompt, b_k_prompt, b_v_prompt, conv_prompt = trunk(
        x_prompt, c_prompt, 0, None, None, None, None, None, None,
        ada_w, ada_b, norm_g, w_qkv_a, lambda_a, subln_g, w_o_a, kv_norm_g, kv_ada_w, kv_ada_b,
        w_kv_b, w_q_b, sinks_b, w_o_b, w_ffn_in, conv_w, conv_b, w_ffn_out)
    y_sample, a_k_sample, a_v_sample, b_k_sample, b_v_sample, conv_sample = trunk(
        x_sample, c_sample, PAST_LEN, cache_a_k, cache_a_v, page_table, state_b_k, state_b_v, state_conv,
        ada_w, ada_b, norm_g, w_qkv_a, lambda_a, subln_g, w_o_a, kv_norm_g, kv_ada_w, kv_ada_b,
        w_kv_b, w_q_b, sinks_b, w_o_b, w_ffn_in, conv_w, conv_b, w_ffn_out)
    return (y_prompt, y_sample, a_k_prompt, a_v_prompt, a_k_sample, a_v_sample,
            b_k_prompt, b_v_prompt, b_k_sample, b_v_sample, conv_prompt, conv_sample)
```

```python
import functools
import math

import jax
import jax.numpy as jnp
import numpy as np
from jax import lax
from jax.experimental import pallas as pl
from jax.experimental.pallas import tpu as pltpu

F32 = jnp.float32
BF16 = jnp.bfloat16

EPS = 1e-6
NEG = -1e30
WINDOW = 128
PAGE = 128
A_HEAD_DIM = 128
A_V_DIM = 256
A_KV_HEADS = 4
A_GROUP = 2
B_HEAD_DIM = 64
B_KV_HEADS = 8
B_GROUP = 4
CONV_W = 3

VMEM_LIMIT_BYTES = 56 * 1024 * 1024


def _params(*semantics):
    return pltpu.CompilerParams(dimension_semantics=semantics, vmem_limit_bytes=VMEM_LIMIT_BYTES)


def _dot(a, b):
    return jnp.dot(a, b, preferred_element_type=F32)


def _dot_nt(a, b):
    return lax.dot_general(a, b, (((1,), (1,)), ((), ())), preferred_element_type=F32)


def _alibi_slopes(n):
    return jnp.asarray(2.0 ** (-8.0 * np.arange(1, n + 1, dtype=np.float32) / n), F32)


def _rms(x, g):
    return (x * lax.rsqrt(jnp.mean(x * x, axis=-1, keepdims=True) + EPS)) * g


def _norm_mod(x, g, shift, scale):
    return _rms(x, g) * (1 + scale) + shift


def _ada_body(c_ref, w_ref, b_ref, o_ref):
    c = c_ref[...]
    s = (c * jax.nn.sigmoid(c)).astype(BF16)
    o_ref[0] = _dot(s, w_ref[0].astype(BF16)) + b_ref[0]


def _ada(c, w, b, tn):
    nl, d, n = w.shape
    r = c.shape[0]
    return pl.pallas_call(
        _ada_body,
        grid=(nl, n // tn),
        in_specs=[
            pl.BlockSpec((r, d), lambda l, j: (0, 0)),
            pl.BlockSpec((1, d, tn), lambda l, j: (l, 0, j)),
            pl.BlockSpec((1, 1, tn), lambda l, j: (l, 0, j)),
        ],
        out_specs=pl.BlockSpec((1, r, tn), lambda l, j: (l, 0, j)),
        out_shape=jax.ShapeDtypeStruct((nl, r, n), F32),
        compiler_params=_params("arbitrary", "arbitrary"),
        name="ada_mod",
    )(c, w, b.reshape(nl, 1, n))


def _mm_mod_body(bounds, x_ref, g_ref, sh_ref, sc_ref, w_ref, *rest):
    outs, h_ref = rest[:-1], rest[-1]
    j = pl.program_id(1)

    @pl.when(j == 0)
    def _():
        h_ref[...] = _norm_mod(x_ref[...], g_ref[...], sh_ref[0], sc_ref[0]).astype(BF16)

    r = _dot(h_ref[...], w_ref[...].astype(BF16))
    if len(outs) == 1:
        outs[0][...] = r
    else:
        for o_ref, (lo, hi) in zip(outs, bounds):
            @pl.when(jnp.logical_and(j >= lo, j < hi))
            def _(o_ref=o_ref):
                o_ref[...] = r


def _mod_spec(mod, tpb):
    return pl.BlockSpec((1,) + mod.shape[1:], lambda i, j: (i // tpb, 0, 0))


def _mm_mod(x, g, shift, scale, w, splits, tm, tn, tpb):
    m, d = x.shape
    n = w.shape[1]
    bounds, lo = [], 0
    for s in splits:
        bounds.append((lo // tn, (lo + s) // tn))
        lo += s
    out_specs = [
        pl.BlockSpec((tm, tn), lambda i, j, lo=lo_, hi=hi_: (i, jnp.clip(j - lo, 0, hi - lo - 1)))
        for lo_, hi_ in bounds
    ]
    out_shape = [jax.ShapeDtypeStruct((m, s), F32) for s in splits]
    return pl.pallas_call(
        functools.partial(_mm_mod_body, bounds),
        grid=(m // tm, n // tn),
        in_specs=[
            pl.BlockSpec((tm, d), lambda i, j: (i, 0)),
            pl.BlockSpec((1, d), lambda i, j: (0, 0)),
            _mod_spec(shift, tpb),
            _mod_spec(scale, tpb),
            pl.BlockSpec((d, tn), lambda i, j: (0, j)),
        ],
        out_specs=out_specs,
        out_shape=out_shape,
        scratch_shapes=[pltpu.VMEM((tm, d), BF16)],
        compiler_params=_params("arbitrary", "arbitrary"),
        name="norm_mod_matmul",
    )(x, g.reshape(1, d), shift, scale, w)


def _proj_res_body(nj, tn, o_ref, w_ref, x_ref, gate_ref, gn_ref, out_ref, y_ref):
    j = pl.program_id(1)
    y_ref[j] = _dot(o_ref[...], w_ref[...].astype(BF16))

    @pl.when(j == nj - 1)
    def _():
        ss = jnp.zeros((y_ref.shape[1], 1), F32)
        for c in range(nj):
            y = y_ref[c]
            ss = ss + jnp.sum(y * y, axis=-1, keepdims=True)
        rs = lax.rsqrt(ss / (nj * tn) + EPS)
        gate = gate_ref[0]
        for c in range(nj):
            cols = slice(c * tn, (c + 1) * tn)
            out_ref[:, cols] = x_ref[:, cols] + gate[:, cols] * ((y_ref[c] * rs) * gn_ref[:, cols])


def _proj_res(o, w, x, gate, gn, tm, tn, tpb):
    m, k = o.shape
    d = w.shape[1]
    nj = d // tn
    return pl.pallas_call(
        functools.partial(_proj_res_body, nj, tn),
        grid=(m // tm, nj),
        in_specs=[
            pl.BlockSpec((tm, k), lambda i, j: (i, 0)),
            pl.BlockSpec((k, tn), lambda i, j: (0, j)),
            pl.BlockSpec((tm, d), lambda i, j: (i, 0)),
            _mod_spec(gate, tpb),
            pl.BlockSpec((1, d), lambda i, j: (0, 0)),
        ],
        out_specs=pl.BlockSpec((tm, d), lambda i, j: (i, 0)),
        out_shape=jax.ShapeDtypeStruct((m, d), F32),
        scratch_shapes=[pltpu.VMEM((nj, tm, tn), F32)],
        compiler_params=_params("arbitrary", "arbitrary"),
        name="proj_residual",
    )(o, w, x, gate, gn.reshape(1, d))


HALO = 16


def _ffn_body(prompt, nf, tm, tpb, *refs):
    if prompt:
        (x_ref, xh_ref, g_ref, sh_ref, sc_ref, gate_ref, gn_ref, wg_ref, wv_ref, cw_ref, cb_ref, wo_ref,
         prev_ref, out_ref, conv_ref, h_ref, hh_ref, acc_ref) = refs
    else:
        (x_ref, g_ref, sh_ref, sc_ref, gate_ref, gn_ref, wg_ref, wv_ref, cw_ref, cb_ref, wo_ref,
         prev_ref, out_ref, conv_ref, h_ref, acc_ref) = refs
    i = pl.program_id(0)
    f = pl.program_id(1)

    @pl.when(f == 0)
    def _():
        h_ref[...] = _norm_mod(x_ref[...], g_ref[...], sh_ref[0], sc_ref[0]).astype(BF16)
        if prompt:
            hh_ref[...] = _norm_mod(xh_ref[...], g_ref[...], sh_ref[0], sc_ref[0]).astype(BF16)
        acc_ref[...] = jnp.zeros_like(acc_ref)

    h = h_ref[...]
    row = lax.broadcasted_iota(jnp.int32, (tm, 1), 0)
    acts = []
    for half, w_ref in enumerate((wg_ref, wv_ref)):
        w = w_ref[...].astype(BF16)
        u = _dot(h, w)
        r1 = pltpu.roll(u, 1, axis=0)
        r2 = pltpu.roll(u, 2, axis=0)
        if prompt:
            uh = _dot(hh_ref[...], w)
            first = (i % tpb) == 0
            m1 = jnp.where(first, prev_ref[0, half, 1:2, :], uh[HALO - 1:HALO, :])
            m2 = jnp.where(first, prev_ref[0, half, 0:1, :], uh[HALO - 2:HALO - 1, :])
            u1 = jnp.where(row >= 1, r1, m1)
            u2 = jnp.where(row >= 2, r2, jnp.where(row == 1, m1, m2))
            conv_ref[0, half] = u[tm - 2:tm, :]
        else:
            t = row % 4
            u1 = jnp.where(t >= 1, r1, prev_ref[0, half])
            u2 = jnp.where(t >= 2, r2, prev_ref[1, half])
            conv_ref[half] = u
        cw = cw_ref[half]
        uc = cb_ref[half] + u2 * cw[0:1, :]
        uc = uc + u1 * cw[1:2, :]
        uc = uc + u * cw[2:3, :]
        acts.append(uc)
    act = (jax.nn.gelu(acts[0], approximate=True) * acts[1]).astype(BF16)
    acc_ref[...] += _dot(act, wo_ref[...].astype(BF16))

    @pl.when(f == nf - 1)
    def _():
        y = acc_ref[...]
        out_ref[...] = x_ref[...] + gate_ref[0] * _rms(y, gn_ref[...])


def _ffn(x, g, shift, scale, gate, gn, w_in, conv_w, conv_b, w_out, prev, prompt, tm, tf, tpb):
    m, d = x.shape
    ff = w_out.shape[0]
    nf = ff // tf
    cw = conv_w.reshape(CONV_W, 2, ff).transpose(1, 0, 2)
    cb = conv_b.reshape(2, 1, ff)
    x_spec = pl.BlockSpec((tm, d), lambda i, f: (i, 0))
    vec_spec = pl.BlockSpec((1, d), lambda i, f: (0, 0))
    in_specs = [x_spec]
    args = [x]
    if prompt:
        in_specs.append(pl.BlockSpec((HALO, d), lambda i, f: (jnp.maximum(i * (tm // HALO) - 1, 0), 0)))
        args.append(x)
    in_specs += [
        vec_spec, _mod_spec(shift, tpb), _mod_spec(scale, tpb), _mod_spec(gate, tpb), vec_spec,
        pl.BlockSpec((d, tf), lambda i, f: (0, f)),
        pl.BlockSpec((d, tf), lambda i, f: (0, f + nf)),
        pl.BlockSpec((2, CONV_W, tf), lambda i, f: (0, 0, f)),
        pl.BlockSpec((2, 1, tf), lambda i, f: (0, 0, f)),
        pl.BlockSpec((tf, d), lambda i, f: (f, 0)),
    ]
    args += [g.reshape(1, d), shift, scale, gate, gn.reshape(1, d), w_in, w_in, cw, cb, w_out]
    scratch = [pltpu.VMEM((tm, d), BF16)]
    if prompt:
        nb = prev.shape[0]
        in_specs.append(pl.BlockSpec((1, 2, 2, tf), lambda i, f: (i // tpb, 0, 0, f)))
        conv_spec = pl.BlockSpec((1, 2, 2, tf), lambda i, f: (i, 0, 0, f))
        conv_shape = jax.ShapeDtypeStruct((m // tm, 2, 2, ff), F32)
        scratch.append(pltpu.VMEM((HALO, d), BF16))
    else:
        in_specs.append(pl.BlockSpec((2, 2, tm, tf), lambda i, f: (0, 0, i, f)))
        conv_spec = pl.BlockSpec((2, tm, tf), lambda i, f: (0, i, f))
        conv_shape = jax.ShapeDtypeStruct((2, m, ff), F32)
    args.append(prev)
    scratch.append(pltpu.VMEM((tm, d), F32))
    return pl.pallas_call(
        functools.partial(_ffn_body, prompt, nf, tm, tpb),
        grid=(m // tm, nf),
        in_specs=in_specs,
        out_specs=[x_spec, conv_spec],
        out_shape=[jax.ShapeDtypeStruct((m, d), F32), conv_shape],
        scratch_shapes=scratch,
        compiler_params=_params("arbitrary", "arbitrary"),
        name="conv_ffn_prompt" if prompt else "conv_ffn_sample",
    )(*args)


def _lambda(lp_ref, lam_init):
    lp = lp_ref[...]
    a = jnp.exp(jnp.sum(lp[0:1, :] * lp[1:2, :], axis=-1, keepdims=True))
    b = jnp.exp(jnp.sum(lp[2:3, :] * lp[3:4, :], axis=-1, keepdims=True))
    return a - b + lam_init


def _subln(o, g, lam_init):
    return _rms(o, g) * (1 - lam_init)


def _attn_a_prompt_body(lam_init, tq, qi_ref, ki_ref, slope_ref, q_ref, k_ref, v_ref, lp_ref, sg_ref, o_ref,
                        qs_ref, m_ref, l_ref, acc_ref):
    n = pl.program_id(1)
    step = pl.program_id(2)
    qi = qi_ref[step]
    ki = ki_ref[step]
    scale = A_HEAD_DIM ** -0.5

    @pl.when(ki == 0)
    def _():
        m_ref[...] = jnp.full_like(m_ref, NEG)
        l_ref[...] = jnp.zeros_like(l_ref)
        acc_ref[...] = jnp.zeros_like(acc_ref)
        for c in range(4):
            qs_ref[c] = (q_ref[:, c * A_HEAD_DIM:(c + 1) * A_HEAD_DIM] * scale).astype(BF16)

    def block(masked):
        vb = v_ref[...].astype(BF16)
        kpos = (lax.broadcasted_iota(jnp.int32, (1, tq), 1) + (ki - qi) * tq).astype(F32)
        if masked:
            future = lax.broadcasted_iota(jnp.int32, (tq, tq), 1) > lax.broadcasted_iota(jnp.int32, (tq, tq), 0)
        for i in range(2):
            kb = k_ref[:, i * A_HEAD_DIM:(i + 1) * A_HEAD_DIM].astype(BF16)
            for g in range(2):
                c = g * 2 + i
                s = _dot_nt(qs_ref[c], kb) + slope_ref[n * A_GROUP + g] * kpos
                if masked:
                    s = jnp.where(future, NEG, s)
                m_prev = m_ref[c]
                m_new = jnp.maximum(m_prev, jnp.max(s, axis=-1, keepdims=True))
                alpha = jnp.exp(m_prev - m_new)
                p = jnp.exp(s - m_new)
                l_ref[c] = alpha * l_ref[c] + jnp.sum(p, axis=-1, keepdims=True)
                acc_ref[c] = alpha * acc_ref[c] + _dot(p.astype(BF16), vb)
                m_ref[c] = m_new

    @pl.when(ki < qi)
    def _():
        block(False)

    @pl.when(ki == qi)
    def _():
        block(True)
        lam = _lambda(lp_ref, lam_init)
        for g in range(2):
            o = acc_ref[g * 2] / l_ref[g * 2] - lam * (acc_ref[g * 2 + 1] / l_ref[g * 2 + 1])
            o_ref[:, g * A_V_DIM:(g + 1) * A_V_DIM] = _subln(o, sg_ref[...], lam_init).astype(BF16)


def _attn_a_prompt(q, k, v, lam_p, subln_g, lam_init, nb, t, tq):
    nq = t // tq
    qi_tab = np.asarray([qi for qi in range(nq) for _ in range(qi + 1)], np.int32)
    ki_tab = np.asarray([ki for qi in range(nq) for ki in range(qi + 1)], np.int32)
    qw = A_GROUP * 2 * A_HEAD_DIM
    grid_spec = pltpu.PrefetchScalarGridSpec(
        num_scalar_prefetch=3,
        grid=(nb, A_KV_HEADS, len(qi_tab)),
        in_specs=[
            pl.BlockSpec((tq, qw), lambda b, n, s, qi, ki, sl: (b * nq + qi[s], n)),
            pl.BlockSpec((tq, 2 * A_HEAD_DIM), lambda b, n, s, qi, ki, sl: (b * nq + ki[s], n)),
            pl.BlockSpec((tq, A_V_DIM), lambda b, n, s, qi, ki, sl: (b * nq + ki[s], n)),
            pl.BlockSpec((4, A_HEAD_DIM), lambda b, n, s, qi, ki, sl: (0, 0)),
            pl.BlockSpec((1, A_V_DIM), lambda b, n, s, qi, ki, sl: (0, 0)),
        ],
        out_specs=pl.BlockSpec((tq, A_GROUP * A_V_DIM), lambda b, n, s, qi, ki, sl: (b * nq + qi[s], n)),
        scratch_shapes=[
            pltpu.VMEM((4, tq, A_HEAD_DIM), BF16),
            pltpu.VMEM((4, tq, 1), F32),
            pltpu.VMEM((4, tq, 1), F32),
            pltpu.VMEM((4, tq, A_V_DIM), F32),
        ],
    )
    return pl.pallas_call(
        functools.partial(_attn_a_prompt_body, lam_init, tq),
        grid_spec=grid_spec,
        out_shape=jax.ShapeDtypeStruct((nb * t, A_KV_HEADS * A_GROUP * A_V_DIM), BF16),
        compiler_params=_params("arbitrary", "arbitrary", "arbitrary"),
        name="diff_attn_prompt",
    )(jnp.asarray(qi_tab), jnp.asarray(ki_tab), _alibi_slopes(A_KV_HEADS * A_GROUP), q, k, v, lam_p,
      subln_g.reshape(1, A_V_DIM))


PAGES_PER_STEP = 8
ROWS_PER_PAGE = PAGE * 8


def _attn_a_decode_body(lam_init, past_len, nchunk, pt_ref, slope_ref, q_ref, kn_ref, vn_ref, lp_ref, sg_ref, *rest):
    npg = PAGES_PER_STEP
    k_refs, v_refs = rest[:npg], rest[npg:2 * npg]
    o_ref, m_ref, l_ref, acc_ref = rest[2 * npg:]
    c = pl.program_id(1)
    scale = A_HEAD_DIM ** -0.5
    nrow = 2 * A_GROUP * 4
    g_row = (lax.broadcasted_iota(jnp.int32, (nrow, 1), 0) // 4) % 2
    t_row = lax.broadcasted_iota(jnp.int32, (nrow, 1), 0) % 4

    @pl.when(c == 0)
    def _():
        m_ref[...] = jnp.full_like(m_ref, NEG)
        l_ref[...] = jnp.zeros_like(l_ref)
        acc_ref[...] = jnp.zeros_like(acc_ref)

    def slope_col(n):
        return jnp.where(g_row == 0, slope_ref[n * A_GROUP], slope_ref[n * A_GROUP + 1])

    kpos = (lax.broadcasted_iota(jnp.int32, (1, npg * PAGE), 1) + (c * (npg * PAGE) - past_len)).astype(F32)
    for n in range(A_KV_HEADS):
        s_rows = []
        for i in range(2):
            qni = (q_ref[0, n * 2 + i] * scale).astype(BF16)
            parts = []
            for r in range(npg):
                kp = k_refs[r][pl.ds(n * 2 + i, PAGE, stride=8), :].astype(BF16)
                parts.append(_dot_nt(qni, kp))
            s_rows.append(jnp.concatenate(parts, axis=1))
        s = jnp.concatenate(s_rows, axis=0) + slope_col(n) * kpos
        m_prev = m_ref[n]
        m_new = jnp.maximum(m_prev, jnp.max(s, axis=-1, keepdims=True))
        alpha = jnp.exp(m_prev - m_new)
        p = jnp.exp(s - m_new)
        l_ref[n] = alpha * l_ref[n] + jnp.sum(p, axis=-1, keepdims=True)
        pb = p.astype(BF16)
        pv = jnp.zeros((nrow, A_V_DIM), F32)
        for r in range(npg):
            vp = jnp.concatenate([v_refs[r][pl.ds(n, PAGE, stride=8), :],
                                  v_refs[r][pl.ds(4 + n, PAGE, stride=8), :]], axis=1).astype(BF16)
            pv = pv + _dot(pb[:, r * PAGE:(r + 1) * PAGE], vp)
        acc_ref[n] = alpha * acc_ref[n] + pv
        m_ref[n] = m_new

    @pl.when(c == nchunk - 1)
    def _():
        lam = _lambda(lp_ref, lam_init)
        for n in range(A_KV_HEADS):
            sl = slope_col(n)
            cols = []
            for tk in range(4):
                parts = []
                for i in range(2):
                    qni = q_ref[0, n * 2 + i] * scale
                    parts.append(jnp.sum(qni * kn_ref[0, n * 2 + i, tk:tk + 1, :], axis=-1, keepdims=True))
                s_tk = jnp.concatenate(parts, axis=0) + sl * float(tk)
                cols.append(jnp.where(t_row >= tk, s_tk, NEG))
            m_prev = m_ref[n]
            m_new = m_prev
            for s_tk in cols:
                m_new = jnp.maximum(m_new, s_tk)
            alpha = jnp.exp(m_prev - m_new)
            l_new = alpha * l_ref[n]
            acc = alpha * acc_ref[n]
            for tk, s_tk in enumerate(cols):
                p_tk = jnp.exp(s_tk - m_new)
                l_new = l_new + p_tk
                acc = acc + p_tk * vn_ref[0, n, tk:tk + 1, :]
            out = acc / l_new
            half = A_GROUP * 4
            o = out[:half] - lam * out[half:]
            o_ref[0, n] = _subln(o, sg_ref[...], lam_init)


def _attn_a_decode(q, kn, vn, ck_rows, cv_rows, page_table, layer, n_pool, lam_p, subln_g, lam_init, past_len):
    nb = q.shape[0]
    n_pages = page_table.shape[1]
    npg = PAGES_PER_STEP
    nchunk = n_pages // npg
    base = layer * n_pool

    def page_spec(r):
        return pl.BlockSpec((ROWS_PER_PAGE, 128), lambda b, c, pt, sl: (pt[b * n_pages + c * npg + r] + base, 0))

    grid_spec = pltpu.PrefetchScalarGridSpec(
        num_scalar_prefetch=2,
        grid=(nb, nchunk),
        in_specs=[
            pl.BlockSpec((1, 8, 8, A_HEAD_DIM), lambda b, c, pt, sl: (b, 0, 0, 0)),
            pl.BlockSpec((1, 8, 4, A_HEAD_DIM), lambda b, c, pt, sl: (b, 0, 0, 0)),
            pl.BlockSpec((1, 4, 4, A_V_DIM), lambda b, c, pt, sl: (b, 0, 0, 0)),
            pl.BlockSpec((4, A_HEAD_DIM), lambda b, c, pt, sl: (0, 0)),
            pl.BlockSpec((1, A_V_DIM), lambda b, c, pt, sl: (0, 0)),
        ] + [page_spec(r) for r in range(npg)] + [page_spec(r) for r in range(npg)],
        out_specs=pl.BlockSpec((1, A_KV_HEADS, A_GROUP * 4, A_V_DIM), lambda b, c, pt, sl: (b, 0, 0, 0)),
        scratch_shapes=[
            pltpu.VMEM((A_KV_HEADS, 16, 1), F32),
            pltpu.VMEM((A_KV_HEADS, 16, 1), F32),
            pltpu.VMEM((A_KV_HEADS, 16, A_V_DIM), F32),
        ],
    )
    return pl.pallas_call(
        functools.partial(_attn_a_decode_body, lam_init, past_len, nchunk),
        grid_spec=grid_spec,
        out_shape=jax.ShapeDtypeStruct((nb, A_KV_HEADS, A_GROUP * 4, A_V_DIM), F32),
        compiler_params=_params("arbitrary", "arbitrary"),
        name="diff_attn_decode",
    )(page_table.reshape(-1), _alibi_slopes(A_KV_HEADS * A_GROUP), q, kn, vn, lam_p, subln_g.reshape(1, A_V_DIM),
      *([ck_rows] * npg), *([cv_rows] * npg))


def _attn_b_body(tq, first_lo_invalid, slope_ref, sink_ref, q_ref, klo_ref, khi_ref, vlo_ref, vhi_ref, o_ref):
    j = pl.program_id(1)
    scale = B_HEAD_DIM ** -0.5
    nk = 2 * WINDOW
    row = lax.broadcasted_iota(jnp.int32, (tq, nk), 0)
    col = lax.broadcasted_iota(jnp.int32, (tq, nk), 1)
    dist = row - col + WINDOW
    valid = jnp.logical_and(dist >= 0, dist < WINDOW)
    if first_lo_invalid:
        valid = jnp.logical_and(valid, jnp.logical_or(col >= WINDOW, j > 0))
    distf = dist.astype(F32)
    for n in range(B_KV_HEADS):
        ksl = slice(n * B_HEAD_DIM, (n + 1) * B_HEAD_DIM)
        kw = jnp.concatenate([klo_ref[0, :, ksl], khi_ref[0, :, ksl]], axis=0).astype(BF16)
        vw = jnp.concatenate([vlo_ref[0, :, ksl], vhi_ref[0, :, ksl]], axis=0).astype(BF16)
        for g in range(B_GROUP):
            h = n * B_GROUP + g
            qh = (q_ref[0, :, h * B_HEAD_DIM:(h + 1) * B_HEAD_DIM] * scale).astype(BF16)
            s = jnp.where(valid, _dot_nt(qh, kw) - slope_ref[h] * distf, NEG)
            sink = sink_ref[h]
            m = jnp.maximum(jnp.max(s, axis=-1, keepdims=True), sink)
            e = jnp.exp(s - m)
            denom = jnp.sum(e, axis=-1, keepdims=True) + jnp.exp(sink - m)
            p = (e / denom).astype(BF16)
            o_ref[0, :, h * B_HEAD_DIM:(h + 1) * B_HEAD_DIM] = _dot(p, vw).astype(BF16)


def _attn_b(q, k, v, sinks, tq, nblk, lo_map, hi_map, first_lo_invalid):
    nbat = q.shape[0] // nblk
    d = q.shape[2]
    kd = k.shape[2]
    grid_spec = pltpu.PrefetchScalarGridSpec(
        num_scalar_prefetch=2,
        grid=(nbat, nblk),
        in_specs=[
            pl.BlockSpec((1, tq, d), lambda b, j, sl, sk: (b * nblk + j, 0, 0)),
            pl.BlockSpec((1, WINDOW, kd), lambda b, j, sl, sk: (lo_map(b, j), 0, 0)),
            pl.BlockSpec((1, WINDOW, kd), lambda b, j, sl, sk: (hi_map(b, j), 0, 0)),
            pl.BlockSpec((1, WINDOW, kd), lambda b, j, sl, sk: (lo_map(b, j), 0, 0)),
            pl.BlockSpec((1, WINDOW, kd), lambda b, j, sl, sk: (hi_map(b, j), 0, 0)),
        ],
        out_specs=pl.BlockSpec((1, tq, d), lambda b, j, sl, sk: (b * nblk + j, 0, 0)),
    )
    return pl.pallas_call(
        functools.partial(_attn_b_body, tq, first_lo_invalid),
        grid_spec=grid_spec,
        out_shape=jax.ShapeDtypeStruct(q.shape, BF16),
        compiler_params=_params("arbitrary", "arbitrary"),
        name="swa_attn",
    )(_alibi_slopes(B_KV_HEADS * B_GROUP), sinks.astype(F32), q, k, k, v, v)


def _lam_init(layer):
    return 0.8 - 0.6 * math.exp(-0.3 * layer)


def _trunk(x, mods, kv_mod, prompt, nb, t, w, cache=None, state_b=None, state_conv=None):
    m, d = x.shape
    depth = len(mods)
    n_a = depth // 2
    ff = w["w_ffn_out"].shape[1]
    if prompt:
        tm, tpb, tq_a, tf = 512, t // 512, 512, 256
    else:
        tm, tpb, tf = m, 1, 256
    a_k, a_v, convs = [], [], []
    kb = vb = None
    for l in range(depth):
        sh1, sc1, g1, sh2, sc2, g2 = mods[l]
        ng = w["norm_g"][l]
        if l < n_a:
            q, k, v = _mm_mod(x, ng[0], sh1, sc1, w["w_qkv_a"][l], (2048, 1024, 1024), tm, 512, tpb)
            a_k.append(k)
            a_v.append(v)
            lam_init = _lam_init(l)
            if prompt:
                o = _attn_a_prompt(q, k, v, w["lambda_a"][l], w["subln_g"][l], lam_init, nb, t, tq_a)
            else:
                ck_rows, cv_rows, page_table, n_pool, past_len = cache
                qd = q.reshape(nb, t, A_KV_HEADS, A_GROUP, 2, A_HEAD_DIM).transpose(0, 2, 4, 3, 1, 5)
                qd = qd.reshape(nb, A_KV_HEADS * 2, A_GROUP * t, A_HEAD_DIM)
                kn = k.reshape(nb, t, A_KV_HEADS * 2, A_HEAD_DIM).transpose(0, 2, 1, 3)
                vn = v.reshape(nb, t, A_KV_HEADS, A_V_DIM).transpose(0, 2, 1, 3)
                od = _attn_a_decode(qd, kn, vn, ck_rows, cv_rows, page_table, l, n_pool, w["lambda_a"][l],
                                    w["subln_g"][l], lam_init, past_len)
                o = od.reshape(nb, A_KV_HEADS, A_GROUP, t, A_V_DIM).transpose(0, 3, 1, 2, 4).reshape(m, d).astype(BF16)
            x = _proj_res(o, w["w_o_a"][l], x, g1, ng[1], tm, 512, tpb)
        else:
            jb = l - n_a
            (q,) = _mm_mod(x, ng[0], sh1, sc1, w["w_q_b"][jb], (d,), tm, 512, tpb)
            kd = B_KV_HEADS * B_HEAD_DIM
            if prompt:
                nblk = t // WINDOW
                o = _attn_b(q.reshape(nb * nblk, WINDOW, d), kb.reshape(nb * nblk, WINDOW, kd),
                            vb.reshape(nb * nblk, WINDOW, kd), w["sinks_b"][jb], WINDOW, nblk,
                            lambda b, j: jnp.maximum(b * nblk + j - 1, 0), lambda b, j: b * nblk + j, True)
            else:
                kcat, vcat = state_b
                o = _attn_b(q.reshape(nb, t, d), kcat.reshape(nb * 2, WINDOW, kd), vcat.reshape(nb * 2, WINDOW, kd),
                            w["sinks_b"][jb], t, 1, lambda b, j: 2 * b, lambda b, j: 2 * b + 1, False)
            x = _proj_res(o.reshape(m, d), w["w_o_b"][jb], x, g1, ng[1], tm, 512, tpb)
        if prompt:
            prev = jnp.zeros((nb, 2, 2, ff), F32)
        else:
            sc = state_conv[l].reshape(nb, 2, 2, ff)
            zero = jnp.zeros((nb, 2, ff), F32)
            p1 = jnp.stack([sc[:, 1], zero, zero, zero], axis=1)
            p2 = jnp.stack([sc[:, 0], sc[:, 1], zero, zero], axis=1)
            prev = jnp.stack([p1, p2]).reshape(2, m, 2, ff).transpose(0, 2, 1, 3)
        x, conv = _ffn(x, ng[2], sh2, sc2, g2, ng[3], w["w_ffn_in"][l], w["conv_w"][l], w["conv_b"][l],
                       w["w_ffn_out"][l], prev, prompt, tm, tf, tpb)
        if prompt:
            convs.append(conv[tpb - 1::tpb].transpose(0, 2, 1, 3).reshape(nb, 2, 2 * ff))
        else:
            u = conv.reshape(2, nb, t, ff)[:, :, t - 2:, :]
            convs.append(u.transpose(1, 2, 0, 3).reshape(nb, 2, 2 * ff))
        if l == n_a - 1:
            kv_shift, kv_scale = kv_mod
            kb, vb = _mm_mod(x, w["kv_norm_g"], kv_shift, kv_scale, w["w_kv_b"], (512, 512), tm, 512, tpb)
            kd = B_KV_HEADS * B_HEAD_DIM
            if prompt:
                new_b = (kb.reshape(nb, t, kd)[:, -WINDOW:], vb.reshape(nb, t, kd)[:, -WINDOW:])
            else:
                sk, sv = state_b
                pad = jnp.zeros((nb, WINDOW - t, kd), F32)
                kcat = jnp.concatenate([sk.reshape(nb, WINDOW, kd), kb.reshape(nb, t, kd), pad], axis=1)
                vcat = jnp.concatenate([sv.reshape(nb, WINDOW, kd), vb.reshape(nb, t, kd), pad], axis=1)
                state_b = (kcat, vcat)
                new_b = (kcat[:, t:t + WINDOW], vcat[:, t:t + WINDOW])
    return x, a_k, a_v, new_b, convs


def kernel(x_prompt, x_sample, c_prompt, c_sample, cache_a_k, cache_a_v, page_table, state_b_k, state_b_v, state_conv, ada_w, ada_b, norm_g, w_qkv_a, lambda_a, subln_g, w_o_a, kv_norm_g, kv_ada_w, kv_ada_b, w_kv_b, w_q_b, sinks_b, w_o_b, w_ffn_in, conv_w, conv_b, w_ffn_out):
    bp, tp, d = x_prompt.shape
    bs, ts, _ = x_sample.shape
    depth = ada_w.shape[0]
    n_a, n_pool = cache_a_k.shape[0], cache_a_k.shape[1]
    past_len = page_table.shape[1] * PAGE
    w = dict(norm_g=norm_g, w_qkv_a=w_qkv_a, lambda_a=lambda_a, subln_g=subln_g, w_o_a=w_o_a, kv_norm_g=kv_norm_g,
             w_kv_b=w_kv_b, w_q_b=w_q_b, sinks_b=sinks_b, w_o_b=w_o_b, w_ffn_in=w_ffn_in, conv_w=conv_w,
             conv_b=conv_b, w_ffn_out=w_ffn_out)

    nrow = bp + bs
    rpad = -nrow % 8
    c_all = jnp.concatenate([c_prompt, c_sample, jnp.zeros((rpad, d), F32)], axis=0)
    mod = _ada(c_all, ada_w, ada_b, 1024)
    kvm = _ada(c_all, kv_ada_w[None], kv_ada_b[None], 1024)[0]

    def prompt_vec(a):
        return a.reshape(bp, 1, d)

    def sample_vec(a):
        return jnp.repeat(a, ts, axis=0).reshape(1, bs * ts, d)

    def split_mods(rows, vec):
        return [[vec(mod[l, rows, i * d:(i + 1) * d]) for i in range(6)] for l in range(depth)]

    p_rows, s_rows = slice(0, bp), slice(bp, bp + bs)
    mods_p, mods_s = split_mods(p_rows, prompt_vec), split_mods(s_rows, sample_vec)
    kv_p = [prompt_vec(kvm[p_rows, i * d:(i + 1) * d]) for i in range(2)]
    kv_s = [sample_vec(kvm[s_rows, i * d:(i + 1) * d]) for i in range(2)]

    yp, akp, avp, nbp, cvp = _trunk(x_prompt.reshape(bp * tp, d), mods_p, kv_p, True, bp, tp, w)

    rows = n_a * n_pool * ROWS_PER_PAGE
    ck_rows = cache_a_k.reshape(rows, 128)
    cv_rows = cache_a_v.reshape(n_a, n_pool, PAGE, A_KV_HEADS, 2, 128).transpose(0, 1, 2, 4, 3, 5).reshape(rows, 128)
    ys, aks, avs, nbs, cvs = _trunk(x_sample.reshape(bs * ts, d), mods_s, kv_s, False, bs, ts, w,
                                    cache=(ck_rows, cv_rows, page_table, n_pool, past_len),
                                    state_b=(state_b_k, state_b_v), state_conv=state_conv)

    def stack_k(ks, b, t):
        return jnp.stack(ks).reshape(n_a, b, t, A_KV_HEADS, 2, A_HEAD_DIM)

    def stack_v(vs, b, t):
        return jnp.stack(vs).reshape(n_a, b, t, A_KV_HEADS, A_V_DIM)

    def bshape(a, b):
        return a.reshape(b, WINDOW, B_KV_HEADS, B_HEAD_DIM)

    return (yp.reshape(bp, tp, d), ys.reshape(bs, ts, d),
            stack_k(akp, bp, tp), stack_v(avp, bp, tp), stack_k(aks, bs, ts), stack_v(avs, bs, ts),
            bshape(nbp[0], bp), bshape(nbp[1], bp), bshape(nbs[0], bs), bshape(nbs[1], bs),
            jnp.stack(cvp), jnp.stack(cvs))
```

```python
import functools
import math

import jax
import jax.numpy as jnp
import numpy as np
from jax import lax
from jax.experimental import pallas as pl
from jax.experimental.pallas import tpu as pltpu

F32 = jnp.float32
BF16 = jnp.bfloat16

EPS = 1e-6
NEG = -1e30
WINDOW = 128
PAGE = 128
A_HEAD_DIM = 128
A_V_DIM = 256
A_KV_HEADS = 4
A_GROUP = 2
B_HEAD_DIM = 64
B_KV_HEADS = 8
B_GROUP = 4
CONV_W = 3

VMEM_LIMIT_BYTES = 56 * 1024 * 1024


def _params(*semantics):
    return pltpu.CompilerParams(dimension_semantics=semantics, vmem_limit_bytes=VMEM_LIMIT_BYTES)


def _dot(a, b):
    return jnp.dot(a, b, preferred_element_type=F32)


def _dot_nt(a, b):
    return lax.dot_general(a, b, (((1,), (1,)), ((), ())), preferred_element_type=F32)


def _alibi_slopes(n):
    return jnp.asarray(2.0 ** (-8.0 * np.arange(1, n + 1, dtype=np.float32) / n), F32)


def _rms(x, g):
    return (x * lax.rsqrt(jnp.mean(x * x, axis=-1, keepdims=True) + EPS)) * g


def _norm_mod(x, g, shift, scale):
    return _rms(x, g) * (1 + scale) + shift


def _ada_body(c_ref, w_ref, b_ref, o_ref):
    c = c_ref[...]
    s = (c * jax.nn.sigmoid(c)).astype(BF16)
    o_ref[0] = _dot(s, w_ref[0].astype(BF16)) + b_ref[0]


def _ada(c, w, b, tn):
    nl, d, n = w.shape
    r = c.shape[0]
    return pl.pallas_call(
        _ada_body,
        grid=(nl, n // tn),
        in_specs=[
            pl.BlockSpec((r, d), lambda l, j: (0, 0)),
            pl.BlockSpec((1, d, tn), lambda l, j: (l, 0, j)),
            pl.BlockSpec((1, 1, tn), lambda l, j: (l, 0, j)),
        ],
        out_specs=pl.BlockSpec((1, r, tn), lambda l, j: (l, 0, j)),
        out_shape=jax.ShapeDtypeStruct((nl, r, n), F32),
        compiler_params=_params("arbitrary", "arbitrary"),
        name="ada_mod",
    )(c, w, b.reshape(nl, 1, n))


def _mm_mod_body(bounds, x_ref, g_ref, sh_ref, sc_ref, w_ref, *rest):
    outs, h_ref = rest[:-1], rest[-1]
    j = pl.program_id(1)

    @pl.when(j == 0)
    def _():
        h_ref[...] = _norm_mod(x_ref[...], g_ref[...], sh_ref[0], sc_ref[0]).astype(BF16)

    r = _dot(h_ref[...], w_ref[...].astype(BF16))
    if len(outs) == 1:
        outs[0][...] = r
    else:
        for o_ref, (lo, hi) in zip(outs, bounds):
            @pl.when(jnp.logical_and(j >= lo, j < hi))
            def _(o_ref=o_ref):
                o_ref[...] = r


def _mod_spec(mod, tpb):
    return pl.BlockSpec((1,) + mod.shape[1:], lambda i, j: (i // tpb, 0, 0))


def _mm_mod(x, g, shift, scale, w, layer, splits, tm, tn, tpb):
    m, d = x.shape
    n = w.shape[2]
    bounds, lo = [], 0
    for s in splits:
        bounds.append((lo // tn, (lo + s) // tn))
        lo += s
    out_specs = [
        pl.BlockSpec((tm, tn), lambda i, j, lo=lo_, hi=hi_: (i, jnp.clip(j - lo, 0, hi - lo - 1)))
        for lo_, hi_ in bounds
    ]
    out_shape = [jax.ShapeDtypeStruct((m, s), F32) for s in splits]
    return pl.pallas_call(
        functools.partial(_mm_mod_body, bounds),
        grid=(m // tm, n // tn),
        in_specs=[
            pl.BlockSpec((tm, d), lambda i, j: (i, 0)),
            pl.BlockSpec((1, d), lambda i, j: (0, 0)),
            _mod_spec(shift, tpb),
            _mod_spec(scale, tpb),
            pl.BlockSpec((None, d, tn), lambda i, j: (layer, 0, j)),
        ],
        out_specs=out_specs,
        out_shape=out_shape,
        scratch_shapes=[pltpu.VMEM((tm, d), BF16)],
        compiler_params=_params("arbitrary", "arbitrary"),
        name="norm_mod_matmul",
    )(x, g.reshape(1, d), shift, scale, w)


def _proj_res_body(nj, tn, o_ref, w_ref, x_ref, gate_ref, gn_ref, out_ref, y_ref):
    j = pl.program_id(1)
    y_ref[j] = _dot(o_ref[...], w_ref[...].astype(BF16))

    @pl.when(j == nj - 1)
    def _():
        ss = jnp.zeros((y_ref.shape[1], 1), F32)
        for c in range(nj):
            y = y_ref[c]
            ss = ss + jnp.sum(y * y, axis=-1, keepdims=True)
        rs = lax.rsqrt(ss / (nj * tn) + EPS)
        gate = gate_ref[0]
        for c in range(nj):
            cols = slice(c * tn, (c + 1) * tn)
            out_ref[:, cols] = x_ref[:, cols] + gate[:, cols] * ((y_ref[c] * rs) * gn_ref[:, cols])


def _proj_res(o, w, layer, x, gate, gn, tm, tn, tpb):
    m, k = o.shape
    d = w.shape[2]
    nj = d // tn
    return pl.pallas_call(
        functools.partial(_proj_res_body, nj, tn),
        grid=(m // tm, nj),
        in_specs=[
            pl.BlockSpec((tm, k), lambda i, j: (i, 0)),
            pl.BlockSpec((None, k, tn), lambda i, j: (layer, 0, j)),
            pl.BlockSpec((tm, d), lambda i, j: (i, 0)),
            _mod_spec(gate, tpb),
            pl.BlockSpec((1, d), lambda i, j: (0, 0)),
        ],
        out_specs=pl.BlockSpec((tm, d), lambda i, j: (i, 0)),
        out_shape=jax.ShapeDtypeStruct((m, d), F32),
        scratch_shapes=[pltpu.VMEM((nj, tm, tn), F32)],
        compiler_params=_params("arbitrary", "arbitrary"),
        name="proj_residual",
    )(o, w, x, gate, gn.reshape(1, d))


HALO = 16
HALO_KEEP = 8


def _ffn_body(prompt, nf, tm, tpb, *refs):
    if prompt:
        (x_ref, xh_ref, g_ref, sh_ref, sc_ref, gate_ref, gn_ref, wg_ref, wv_ref, cw_ref, cb_ref, wo_ref,
         prev_ref, out_ref, conv_ref, h_ref, acc_ref, u_ref, act_ref, hh_ref, uh_ref) = refs
    else:
        (x_ref, g_ref, sh_ref, sc_ref, gate_ref, gn_ref, wg_ref, wv_ref, cw_ref, cb_ref, wo_ref,
         prev_ref, out_ref, conv_ref, h_ref, acc_ref, u_ref, act_ref) = refs
    i = pl.program_id(0)
    f = pl.program_id(1)

    @pl.when(f == 0)
    def _():
        h_ref[...] = _norm_mod(x_ref[...], g_ref[...], sh_ref[0], sc_ref[0]).astype(BF16)
        acc_ref[...] = jnp.zeros_like(acc_ref)
        u_ref[...] = jnp.zeros_like(u_ref)
        act_ref[...] = jnp.zeros_like(act_ref)
        if prompt:
            hh_ref[...] = _norm_mod(xh_ref[...], g_ref[...], sh_ref[0], sc_ref[0]).astype(BF16)
            uh_ref[...] = jnp.zeros_like(uh_ref)

    def down_project():
        acc_ref[...] += _dot(act_ref[...], wo_ref[...].astype(BF16))

    def up_project():
        out = []
        for w_ref in (wg_ref, wv_ref):
            w = w_ref[...].astype(BF16)
            nu = _dot(h_ref[...], w)
            nuh = _dot(hh_ref[...], w)[HALO - HALO_KEEP:, :] if prompt else None
            out.append((nu, nuh))
        return out

    def conv_gate():
        row = lax.broadcasted_iota(jnp.int32, (tm, 1), 0)
        acts = []
        for half in range(2):
            u = u_ref[half]
            r1 = pltpu.roll(u, 1, axis=0)
            r2 = pltpu.roll(u, 2, axis=0)
            if prompt:
                uh = uh_ref[half]
                first = (i % tpb) == 0
                m1 = jnp.where(first, prev_ref[0, half, 1:2, :], uh[HALO_KEEP - 1:HALO_KEEP, :])
                m2 = jnp.where(first, prev_ref[0, half, 0:1, :], uh[HALO_KEEP - 2:HALO_KEEP - 1, :])
                u1 = jnp.where(row >= 1, r1, m1)
                u2 = jnp.where(row >= 2, r2, jnp.where(row == 1, m1, m2))
                conv_ref[0, half] = u[tm - 2:tm, :]
            else:
                t = row % 4
                u1 = jnp.where(t >= 1, r1, prev_ref[0, half])
                u2 = jnp.where(t >= 2, r2, prev_ref[1, half])
                conv_ref[half] = u
            cw = cw_ref[half]
            uc = cb_ref[half] + u2 * cw[0:1, :]
            uc = uc + u1 * cw[1:2, :]
            uc = uc + u * cw[2:3, :]
            acts.append(uc)
        live = (f > 0).astype(F32)
        return ((jax.nn.gelu(acts[0], approximate=True) * acts[1]) * live).astype(BF16)

    @pl.when(f < nf)
    def _():
        down_project()
        new = up_project()
        act_ref[...] = conv_gate()
        for half, (nu, nuh) in enumerate(new):
            u_ref[half] = nu
            if prompt:
                uh_ref[half] = nuh

    @pl.when(f >= nf)
    def _():
        down_project()
        act_ref[...] = conv_gate()

    @pl.when(f == nf + 1)
    def _():
        y = acc_ref[...]
        out_ref[...] = x_ref[...] + gate_ref[0] * _rms(y, gn_ref[...])


def _ffn(x, g, shift, scale, gate, gn, w_in, conv_w, conv_b, w_out, layer, prev, prompt, tm, tf, tpb):
    m, d = x.shape
    ff = w_out.shape[1]
    nf = ff // tf
    cw = conv_w.reshape(CONV_W, 2, ff).transpose(1, 0, 2)
    cb = conv_b.reshape(2, 1, ff)

    def lag(f, k):
        return jnp.clip(f - k, 0, nf - 1)

    x_spec = pl.BlockSpec((tm, d), lambda i, f: (i, 0))
    vec_spec = pl.BlockSpec((1, d), lambda i, f: (0, 0))
    in_specs = [x_spec]
    args = [x]
    if prompt:
        in_specs.append(pl.BlockSpec((HALO, d), lambda i, f: (jnp.maximum(i * (tm // HALO) - 1, 0), 0)))
        args.append(x)
    in_specs += [
        vec_spec, _mod_spec(shift, tpb), _mod_spec(scale, tpb), _mod_spec(gate, tpb), vec_spec,
        pl.BlockSpec((None, d, tf), lambda i, f: (layer, 0, lag(f, 0))),
        pl.BlockSpec((None, d, tf), lambda i, f: (layer, 0, lag(f, 0) + nf)),
        pl.BlockSpec((2, CONV_W, tf), lambda i, f: (0, 0, lag(f, 1))),
        pl.BlockSpec((2, 1, tf), lambda i, f: (0, 0, lag(f, 1))),
        pl.BlockSpec((None, tf, d), lambda i, f: (layer, lag(f, 2), 0)),
    ]
    args += [g.reshape(1, d), shift, scale, gate, gn.reshape(1, d), w_in, w_in, cw, cb, w_out]
    scratch = [pltpu.VMEM((tm, d), BF16), pltpu.VMEM((tm, d), F32), pltpu.VMEM((2, tm, tf), F32),
               pltpu.VMEM((tm, tf), BF16)]
    if prompt:
        in_specs.append(pl.BlockSpec((1, 2, 2, tf), lambda i, f: (i // tpb, 0, 0, lag(f, 1))))
        conv_spec = pl.BlockSpec((1, 2, 2, tf), lambda i, f: (i, 0, 0, lag(f, 1)))
        conv_shape = jax.ShapeDtypeStruct((m // tm, 2, 2, ff), F32)
        scratch += [pltpu.VMEM((HALO, d), BF16), pltpu.VMEM((2, HALO_KEEP, tf), F32)]
    else:
        in_specs.append(pl.BlockSpec((2, 2, tm, tf), lambda i, f: (0, 0, i, lag(f, 1))))
        conv_spec = pl.BlockSpec((2, tm, tf), lambda i, f: (0, i, lag(f, 1)))
        conv_shape = jax.ShapeDtypeStruct((2, m, ff), F32)
    args.append(prev)
    return pl.pallas_call(
        functools.partial(_ffn_body, prompt, nf, tm, tpb),
        grid=(m // tm, nf + 2),
        in_specs=in_specs,
        out_specs=[x_spec, conv_spec],
        out_shape=[jax.ShapeDtypeStruct((m, d), F32), conv_shape],
        scratch_shapes=scratch,
        compiler_params=_params("arbitrary", "arbitrary"),
        name="conv_ffn_prompt" if prompt else "conv_ffn_sample",
    )(*args)


def _lambda(lp_ref, lam_init):
    lp = lp_ref[...]
    a = jnp.exp(jnp.sum(lp[0:1, :] * lp[1:2, :], axis=-1, keepdims=True))
    b = jnp.exp(jnp.sum(lp[2:3, :] * lp[3:4, :], axis=-1, keepdims=True))
    return a - b + lam_init


def _subln(o, g, lam_init):
    return _rms(o, g) * (1 - lam_init)


def _attn_a_prompt_body(lam_init, tq, qi_ref, ki_ref, slope_ref, q_ref, k_ref, v_ref, lp_ref, sg_ref, o_ref,
                        qs_ref, m_ref, l_ref, acc_ref):
    n = pl.program_id(1)
    step = pl.program_id(2)
    qi = qi_ref[step]
    ki = ki_ref[step]
    scale = A_HEAD_DIM ** -0.5

    @pl.when(ki == 0)
    def _():
        m_ref[...] = jnp.full_like(m_ref, NEG)
        l_ref[...] = jnp.zeros_like(l_ref)
        acc_ref[...] = jnp.zeros_like(acc_ref)
        for c in range(4):
            qs_ref[c] = (q_ref[:, c * A_HEAD_DIM:(c + 1) * A_HEAD_DIM] * scale).astype(BF16)

    def block(masked):
        vb = v_ref[...].astype(BF16)
        kpos = (lax.broadcasted_iota(jnp.int32, (1, tq), 1) + (ki - qi) * tq).astype(F32)
        if masked:
            future = lax.broadcasted_iota(jnp.int32, (tq, tq), 1) > lax.broadcasted_iota(jnp.int32, (tq, tq), 0)
        for i in range(2):
            kb = k_ref[:, i * A_HEAD_DIM:(i + 1) * A_HEAD_DIM].astype(BF16)
            for g in range(2):
                c = g * 2 + i
                s = _dot_nt(qs_ref[c], kb) + slope_ref[n * A_GROUP + g] * kpos
                if masked:
                    s = jnp.where(future, NEG, s)
                m_prev = m_ref[c]
                m_new = jnp.maximum(m_prev, jnp.max(s, axis=-1, keepdims=True))
                alpha = jnp.exp(m_prev - m_new)
                p = jnp.exp(s - m_new)
                l_ref[c] = alpha * l_ref[c] + jnp.sum(p, axis=-1, keepdims=True)
                acc_ref[c] = alpha * acc_ref[c] + _dot(p.astype(BF16), vb)
                m_ref[c] = m_new

    @pl.when(ki < qi)
    def _():
        block(False)

    @pl.when(ki == qi)
    def _():
        block(True)
        lam = _lambda(lp_ref, lam_init)
        for g in range(2):
            o = acc_ref[g * 2] / l_ref[g * 2] - lam * (acc_ref[g * 2 + 1] / l_ref[g * 2 + 1])
            o_ref[:, g * A_V_DIM:(g + 1) * A_V_DIM] = _subln(o, sg_ref[...], lam_init).astype(BF16)


def _attn_a_prompt(q, k, v, lam_p, subln_g, lam_init, nb, t, tq):
    nq = t // tq
    qi_tab = np.asarray([qi for qi in range(nq) for _ in range(qi + 1)], np.int32)
    ki_tab = np.asarray([ki for qi in range(nq) for ki in range(qi + 1)], np.int32)
    qw = A_GROUP * 2 * A_HEAD_DIM
    grid_spec = pltpu.PrefetchScalarGridSpec(
        num_scalar_prefetch=3,
        grid=(nb, A_KV_HEADS, len(qi_tab)),
        in_specs=[
            pl.BlockSpec((tq, qw), lambda b, n, s, qi, ki, sl: (b * nq + qi[s], n)),
            pl.BlockSpec((tq, 2 * A_HEAD_DIM), lambda b, n, s, qi, ki, sl: (b * nq + ki[s], n)),
            pl.BlockSpec((tq, A_V_DIM), lambda b, n, s, qi, ki, sl: (b * nq + ki[s], n)),
            pl.BlockSpec((4, A_HEAD_DIM), lambda b, n, s, qi, ki, sl: (0, 0)),
            pl.BlockSpec((1, A_V_DIM), lambda b, n, s, qi, ki, sl: (0, 0)),
        ],
        out_specs=pl.BlockSpec((tq, A_GROUP * A_V_DIM), lambda b, n, s, qi, ki, sl: (b * nq + qi[s], n)),
        scratch_shapes=[
            pltpu.VMEM((4, tq, A_HEAD_DIM), BF16),
            pltpu.VMEM((4, tq, 1), F32),
            pltpu.VMEM((4, tq, 1), F32),
            pltpu.VMEM((4, tq, A_V_DIM), F32),
        ],
    )
    return pl.pallas_call(
        functools.partial(_attn_a_prompt_body, lam_init, tq),
        grid_spec=grid_spec,
        out_shape=jax.ShapeDtypeStruct((nb * t, A_KV_HEADS * A_GROUP * A_V_DIM), BF16),
        compiler_params=_params("arbitrary", "arbitrary", "arbitrary"),
        name="diff_attn_prompt",
    )(jnp.asarray(qi_tab), jnp.asarray(ki_tab), _alibi_slopes(A_KV_HEADS * A_GROUP), q, k, v, lam_p,
      subln_g.reshape(1, A_V_DIM))


PAGES_PER_STEP = 16
ROWS_PER_PAGE = PAGE * 8


def _attn_a_decode_body(lam_init, past_len, nchunk, pt_ref, slope_ref, q_ref, kn_ref, vn_ref, lp_ref, sg_ref, *rest):
    npg = PAGES_PER_STEP
    k_refs, v_refs = rest[:npg], rest[npg:2 * npg]
    o_ref, m_ref, l_ref, acc_ref = rest[2 * npg:]
    c = pl.program_id(1)
    scale = A_HEAD_DIM ** -0.5
    nrow = 2 * A_GROUP * 4
    g_row = (lax.broadcasted_iota(jnp.int32, (nrow, 1), 0) // 4) % 2
    t_row = lax.broadcasted_iota(jnp.int32, (nrow, 1), 0) % 4

    @pl.when(c == 0)
    def _():
        m_ref[...] = jnp.full_like(m_ref, NEG)
        l_ref[...] = jnp.zeros_like(l_ref)
        acc_ref[...] = jnp.zeros_like(acc_ref)

    def slope_col(n):
        return jnp.where(g_row == 0, slope_ref[n * A_GROUP], slope_ref[n * A_GROUP + 1])

    kpos = (lax.broadcasted_iota(jnp.int32, (1, npg * PAGE), 1) + (c * (npg * PAGE) - past_len)).astype(F32)
    for n in range(A_KV_HEADS):
        s_rows = []
        for i in range(2):
            qni = (q_ref[0, n * 2 + i] * scale).astype(BF16)
            parts = []
            for r in range(npg):
                kp = k_refs[r][pl.ds(n * 2 + i, PAGE, stride=8), :].astype(BF16)
                parts.append(_dot_nt(qni, kp))
            s_rows.append(jnp.concatenate(parts, axis=1))
        s = jnp.concatenate(s_rows, axis=0) + slope_col(n) * kpos
        m_prev = m_ref[n]
        m_new = jnp.maximum(m_prev, jnp.max(s, axis=-1, keepdims=True))
        alpha = jnp.exp(m_prev - m_new)
        p = jnp.exp(s - m_new)
        l_ref[n] = alpha * l_ref[n] + jnp.sum(p, axis=-1, keepdims=True)
        pb = p.astype(BF16)
        pv = jnp.zeros((nrow, A_V_DIM), F32)
        for r in range(npg):
            vp = jnp.concatenate([v_refs[r][pl.ds(n, PAGE, stride=8), :],
                                  v_refs[r][pl.ds(4 + n, PAGE, stride=8), :]], axis=1).astype(BF16)
            pv = pv + _dot(pb[:, r * PAGE:(r + 1) * PAGE], vp)
        acc_ref[n] = alpha * acc_ref[n] + pv
        m_ref[n] = m_new

    @pl.when(c == nchunk - 1)
    def _():
        lam = _lambda(lp_ref, lam_init)
        for n in range(A_KV_HEADS):
            sl = slope_col(n)
            cols = []
            for tk in range(4):
                parts = []
                for i in range(2):
                    qni = q_ref[0, n * 2 + i] * scale
                    parts.append(jnp.sum(qni * kn_ref[0, n * 2 + i, tk:tk + 1, :], axis=-1, keepdims=True))
                s_tk = jnp.concatenate(parts, axis=0) + sl * float(tk)
                cols.append(jnp.where(t_row >= tk, s_tk, NEG))
            m_prev = m_ref[n]
            m_new = m_prev
            for s_tk in cols:
                m_new = jnp.maximum(m_new, s_tk)
            alpha = jnp.exp(m_prev - m_new)
            l_new = alpha * l_ref[n]
            acc = alpha * acc_ref[n]
            for tk, s_tk in enumerate(cols):
                p_tk = jnp.exp(s_tk - m_new)
                l_new = l_new + p_tk
                acc = acc + p_tk * vn_ref[0, n, tk:tk + 1, :]
            out = acc / l_new
            half = A_GROUP * 4
            o = out[:half] - lam * out[half:]
            o_ref[0, n] = _subln(o, sg_ref[...], lam_init)


def _attn_a_decode(q, kn, vn, ck_rows, cv_rows, page_table, layer, n_pool, lam_p, subln_g, lam_init, past_len):
    nb = q.shape[0]
    n_pages = page_table.shape[1]
    npg = PAGES_PER_STEP
    nchunk = n_pages // npg
    base = layer * n_pool

    def page_spec(r):
        return pl.BlockSpec((ROWS_PER_PAGE, 128), lambda b, c, pt, sl: (pt[b * n_pages + c * npg + r] + base, 0))

    grid_spec = pltpu.PrefetchScalarGridSpec(
        num_scalar_prefetch=2,
        grid=(nb, nchunk),
        in_specs=[
            pl.BlockSpec((1, 8, 8, A_HEAD_DIM), lambda b, c, pt, sl: (b, 0, 0, 0)),
            pl.BlockSpec((1, 8, 4, A_HEAD_DIM), lambda b, c, pt, sl: (b, 0, 0, 0)),
            pl.BlockSpec((1, 4, 4, A_V_DIM), lambda b, c, pt, sl: (b, 0, 0, 0)),
            pl.BlockSpec((4, A_HEAD_DIM), lambda b, c, pt, sl: (0, 0)),
            pl.BlockSpec((1, A_V_DIM), lambda b, c, pt, sl: (0, 0)),
        ] + [page_spec(r) for r in range(npg)] + [page_spec(r) for r in range(npg)],
        out_specs=pl.BlockSpec((1, A_KV_HEADS, A_GROUP * 4, A_V_DIM), lambda b, c, pt, sl: (b, 0, 0, 0)),
        scratch_shapes=[
            pltpu.VMEM((A_KV_HEADS, 16, 1), F32),
            pltpu.VMEM((A_KV_HEADS, 16, 1), F32),
            pltpu.VMEM((A_KV_HEADS, 16, A_V_DIM), F32),
        ],
    )
    return pl.pallas_call(
        functools.partial(_attn_a_decode_body, lam_init, past_len, nchunk),
        grid_spec=grid_spec,
        out_shape=jax.ShapeDtypeStruct((nb, A_KV_HEADS, A_GROUP * 4, A_V_DIM), F32),
        compiler_params=_params("arbitrary", "arbitrary"),
        name="diff_attn_decode",
    )(page_table.reshape(-1), _alibi_slopes(A_KV_HEADS * A_GROUP), q, kn, vn, lam_p, subln_g.reshape(1, A_V_DIM),
      *([ck_rows] * npg), *([cv_rows] * npg))


def _attn_b_body(rows, ntok, nseq, first_lo_invalid, nkb, slope_ref, sink_ref, q_ref, *rest):
    k_refs, v_refs, o_ref = rest[:nkb], rest[nkb:2 * nkb], rest[2 * nkb]
    j = pl.program_id(1)
    scale = B_HEAD_DIM ** -0.5
    nr = B_GROUP * rows
    nk = nseq * 2 * WINDOW
    r = lax.broadcasted_iota(jnp.int32, (nr, nk), 0) % rows
    c = lax.broadcasted_iota(jnp.int32, (nr, nk), 1)
    kk = c % (2 * WINDOW)
    dist = r % ntok - kk + WINDOW
    valid = jnp.logical_and(dist >= 0, dist < WINDOW)
    if nseq > 1:
        valid = jnp.logical_and(valid, r // ntok == c // (2 * WINDOW))
    if first_lo_invalid:
        valid = jnp.logical_and(valid, jnp.logical_or(kk >= WINDOW, j > 0))
    distf = dist.astype(F32)
    g_col = lax.broadcasted_iota(jnp.int32, (nr, 1), 0) // rows

    def per_row(ref, n):
        out = jnp.zeros((nr, 1), F32)
        for g in range(B_GROUP):
            out = jnp.where(g_col == g, ref[n * B_GROUP + g], out)
        return out

    for n in range(B_KV_HEADS):
        ksl = slice(n * B_HEAD_DIM, (n + 1) * B_HEAD_DIM)
        kw = jnp.concatenate([kr[0, :, ksl] for kr in k_refs], axis=0).astype(BF16)
        vw = jnp.concatenate([vr[0, :, ksl] for vr in v_refs], axis=0).astype(BF16)
        q4 = jnp.concatenate([q_ref[0, :, (n * B_GROUP + g) * B_HEAD_DIM:(n * B_GROUP + g + 1) * B_HEAD_DIM]
                              for g in range(B_GROUP)], axis=0)
        s = jnp.where(valid, _dot_nt((q4 * scale).astype(BF16), kw) - per_row(slope_ref, n) * distf, NEG)
        sink = per_row(sink_ref, n)
        m = jnp.maximum(jnp.max(s, axis=-1, keepdims=True), sink)
        e = jnp.exp(s - m)
        denom = jnp.sum(e, axis=-1, keepdims=True) + jnp.exp(sink - m)
        p = (e * (1.0 / denom)).astype(BF16)
        o = _dot(p, vw)
        for g in range(B_GROUP):
            h = n * B_GROUP + g
            o_ref[0, :, h * B_HEAD_DIM:(h + 1) * B_HEAD_DIM] = o[g * rows:(g + 1) * rows].astype(BF16)


def _attn_b(q, k, v, sinks, ntok, nseq, nblk, key_maps, first_lo_invalid):
    nbat = q.shape[0] // nblk
    rows, d = q.shape[1], q.shape[2]
    kr, kd = k.shape[1], k.shape[2]
    nkb = len(key_maps)
    kv_specs = [pl.BlockSpec((1, kr, kd), lambda b, j, sl, sk, km=km: (km(b, j), 0, 0)) for km in key_maps]
    grid_spec = pltpu.PrefetchScalarGridSpec(
        num_scalar_prefetch=2,
        grid=(nbat, nblk),
        in_specs=[pl.BlockSpec((1, rows, d), lambda b, j, sl, sk: (b * nblk + j, 0, 0))] + kv_specs + kv_specs,
        out_specs=pl.BlockSpec((1, rows, d), lambda b, j, sl, sk: (b * nblk + j, 0, 0)),
    )
    return pl.pallas_call(
        functools.partial(_attn_b_body, rows, ntok, nseq, first_lo_invalid, nkb),
        grid_spec=grid_spec,
        out_shape=jax.ShapeDtypeStruct(q.shape, BF16),
        compiler_params=_params("arbitrary", "arbitrary"),
        name="swa_attn",
    )(_alibi_slopes(B_KV_HEADS * B_GROUP), sinks.astype(F32), q, *([k] * nkb), *([v] * nkb))


SWA_SEQS_PER_TILE = 8


def _lam_init(layer):
    return 0.8 - 0.6 * math.exp(-0.3 * layer)


def _trunk(x, mods, kv_mod, prompt, nb, t, w, cache=None, state_b=None, state_conv=None):
    m, d = x.shape
    depth = len(mods)
    n_a = depth // 2
    ff = w["w_ffn_out"].shape[1]
    if prompt:
        tm_mm, tm, tq_a, tf = 1024, 512, 512, 256
    else:
        tm_mm, tm, tf = m, m, 256
    tpb_mm, tpb = max(t // tm_mm, 1), max(t // tm, 1)
    kd = B_KV_HEADS * B_HEAD_DIM
    a_k, a_v, convs = [], [], []
    kb = vb = None
    for l in range(depth):
        sh1, sc1, g1, sh2, sc2, g2 = mods[l]
        ng = w["norm_g"][l]
        if l < n_a:
            q, k, v = _mm_mod(x, ng[0], sh1, sc1, w["w_qkv_a"], l, (2048, 1024, 1024), tm_mm, 512, tpb_mm)
            a_k.append(k)
            a_v.append(v)
            lam_init = _lam_init(l)
            if prompt:
                o = _attn_a_prompt(q, k, v, w["lambda_a"][l], w["subln_g"][l], lam_init, nb, t, tq_a)
            else:
                ck_rows, cv_rows, page_table, n_pool, past_len = cache
                qd = q.reshape(nb, t, A_KV_HEADS, A_GROUP, 2, A_HEAD_DIM).transpose(0, 2, 4, 3, 1, 5)
                qd = qd.reshape(nb, A_KV_HEADS * 2, A_GROUP * t, A_HEAD_DIM)
                kn = k.reshape(nb, t, A_KV_HEADS * 2, A_HEAD_DIM).transpose(0, 2, 1, 3)
                vn = v.reshape(nb, t, A_KV_HEADS, A_V_DIM).transpose(0, 2, 1, 3)
                od = _attn_a_decode(qd, kn, vn, ck_rows, cv_rows, page_table, l, n_pool, w["lambda_a"][l],
                                    w["subln_g"][l], lam_init, past_len)
                o = od.reshape(nb, A_KV_HEADS, A_GROUP, t, A_V_DIM).transpose(0, 3, 1, 2, 4).reshape(m, d).astype(BF16)
            x = _proj_res(o, w["w_o_a"], l, x, g1, ng[1], tm, 512, tpb)
        else:
            jb = l - n_a
            (q,) = _mm_mod(x, ng[0], sh1, sc1, w["w_q_b"], jb, (d,), tm_mm, 512, tpb_mm)
            if prompt:
                nblk = t // WINDOW
                o = _attn_b(q.reshape(nb * nblk, WINDOW, d), kb.reshape(nb * nblk, WINDOW, kd),
                            vb.reshape(nb * nblk, WINDOW, kd), w["sinks_b"][jb], WINDOW, 1, nblk,
                            (lambda b, j: jnp.maximum(b * nblk + j - 1, 0), lambda b, j: b * nblk + j), True)
            else:
                kcat, vcat = state_b
                ns = SWA_SEQS_PER_TILE
                o = _attn_b(q.reshape(nb // ns, ns * t, d), kcat.reshape(nb // ns, ns * 2 * WINDOW, kd),
                            vcat.reshape(nb // ns, ns * 2 * WINDOW, kd), w["sinks_b"][jb], t, ns, 1,
                            (lambda b, j: b,), False)
            x = _proj_res(o.reshape(m, d), w["w_o_b"], jb, x, g1, ng[1], tm, 512, tpb)
        if prompt:
            prev = jnp.zeros((nb, 2, 2, ff), F32)
        else:
            sc = state_conv[l].reshape(nb, 2, 2, ff)
            zero = jnp.zeros((nb, 2, ff), F32)
            p1 = jnp.stack([sc[:, 1], zero, zero, zero], axis=1)
            p2 = jnp.stack([sc[:, 0], sc[:, 1], zero, zero], axis=1)
            prev = jnp.stack([p1, p2]).reshape(2, m, 2, ff).transpose(0, 2, 1, 3)
        x, conv = _ffn(x, ng[2], sh2, sc2, g2, ng[3], w["w_ffn_in"], w["conv_w"][l], w["conv_b"][l],
                       w["w_ffn_out"], l, prev, prompt, tm, tf, tpb)
        if prompt:
            convs.append(conv[tpb - 1::tpb].transpose(0, 2, 1, 3).reshape(nb, 2, 2 * ff))
        else:
            u = conv.reshape(2, nb, t, ff)[:, :, t - 2:, :]
            convs.append(u.transpose(1, 2, 0, 3).reshape(nb, 2, 2 * ff))
        if l == n_a - 1:
            kv_shift, kv_scale = kv_mod
            kb, vb = _mm_mod(x, w["kv_norm_g"], kv_shift, kv_scale, w["w_kv_b"][None], 0, (kd, kd), tm_mm, 512, tpb_mm)
            if prompt:
                new_b = (kb.reshape(nb, t, kd)[:, -WINDOW:], vb.reshape(nb, t, kd)[:, -WINDOW:])
            else:
                sk, sv = state_b
                pad = jnp.zeros((nb, WINDOW - t, kd), F32)
                kcat = jnp.concatenate([sk.reshape(nb, WINDOW, kd), kb.reshape(nb, t, kd), pad], axis=1)
                vcat = jnp.concatenate([sv.reshape(nb, WINDOW, kd), vb.reshape(nb, t, kd), pad], axis=1)
                state_b = (kcat, vcat)
                new_b = (kcat[:, t:t + WINDOW], vcat[:, t:t + WINDOW])
    return x, a_k, a_v, new_b, convs


def kernel(x_prompt, x_sample, c_prompt, c_sample, cache_a_k, cache_a_v, page_table, state_b_k, state_b_v, state_conv, ada_w, ada_b, norm_g, w_qkv_a, lambda_a, subln_g, w_o_a, kv_norm_g, kv_ada_w, kv_ada_b, w_kv_b, w_q_b, sinks_b, w_o_b, w_ffn_in, conv_w, conv_b, w_ffn_out):
    bp, tp, d = x_prompt.shape
    bs, ts, _ = x_sample.shape
    depth = ada_w.shape[0]
    n_a, n_pool = cache_a_k.shape[0], cache_a_k.shape[1]
    past_len = page_table.shape[1] * PAGE
    w = dict(norm_g=norm_g, w_qkv_a=w_qkv_a, lambda_a=lambda_a, subln_g=subln_g, w_o_a=w_o_a, kv_norm_g=kv_norm_g,
             w_kv_b=w_kv_b, w_q_b=w_q_b, sinks_b=sinks_b, w_o_b=w_o_b, w_ffn_in=w_ffn_in, conv_w=conv_w,
             conv_b=conv_b, w_ffn_out=w_ffn_out)

    nrow = bp + bs
    rpad = -nrow % 8
    c_all = jnp.concatenate([c_prompt, c_sample, jnp.zeros((rpad, d), F32)], axis=0)
    mod = _ada(c_all, ada_w, ada_b, 1024)
    kvm = _ada(c_all, kv_ada_w[None], kv_ada_b[None], 1024)[0]

    def prompt_vec(a):
        return a.reshape(bp, 1, d)

    def sample_vec(a):
        return jnp.repeat(a, ts, axis=0).reshape(1, bs * ts, d)

    def split_mods(rows, vec):
        return [[vec(mod[l, rows, i * d:(i + 1) * d]) for i in range(6)] for l in range(depth)]

    p_rows, s_rows = slice(0, bp), slice(bp, bp + bs)
    mods_p, mods_s = split_mods(p_rows, prompt_vec), split_mods(s_rows, sample_vec)
    kv_p = [prompt_vec(kvm[p_rows, i * d:(i + 1) * d]) for i in range(2)]
    kv_s = [sample_vec(kvm[s_rows, i * d:(i + 1) * d]) for i in range(2)]

    yp, akp, avp, nbp, cvp = _trunk(x_prompt.reshape(bp * tp, d), mods_p, kv_p, True, bp, tp, w)

    rows = n_a * n_pool * ROWS_PER_PAGE
    ck_rows = cache_a_k.reshape(rows, 128)
    cv_rows = cache_a_v.reshape(n_a, n_pool, PAGE, A_KV_HEADS, 2, 128).transpose(0, 1, 2, 4, 3, 5).reshape(rows, 128)
    ys, aks, avs, nbs, cvs = _trunk(x_sample.reshape(bs * ts, d), mods_s, kv_s, False, bs, ts, w,
                                    cache=(ck_rows, cv_rows, page_table, n_pool, past_len),
                                    state_b=(state_b_k, state_b_v), state_conv=state_conv)

    def stack_k(ks, b, t):
        return jnp.stack(ks).reshape(n_a, b, t, A_KV_HEADS, 2, A_HEAD_DIM)

    def stack_v(vs, b, t):
        return jnp.stack(vs).reshape(n_a, b, t, A_KV_HEADS, A_V_DIM)

    def bshape(a, b):
        return a.reshape(b, WINDOW, B_KV_HEADS, B_HEAD_DIM)

    return (yp.reshape(bp, tp, d), ys.reshape(bs, ts, d),
            stack_k(akp, bp, tp), stack_v(avp, bp, tp), stack_k(aks, bs, ts), stack_v(avs, bs, ts),
            bshape(nbp[0], bp), bshape(nbp[1], bp), bshape(nbs[0], bs), bshape(nbs[1], bs),
            jnp.stack(cvp), jnp.stack(cvs))
```

```python
import functools
import math

import jax
import jax.numpy as jnp
import numpy as np
from jax import lax
from jax.experimental import pallas as pl
from jax.experimental.pallas import tpu as pltpu

F32 = jnp.float32
BF16 = jnp.bfloat16

EPS = 1e-6
NEG = -1e30
WINDOW = 128
PAGE = 128
A_HEAD_DIM = 128
A_V_DIM = 256
A_KV_HEADS = 4
A_GROUP = 2
B_HEAD_DIM = 64
B_KV_HEADS = 8
B_GROUP = 4
CONV_W = 3

VMEM_LIMIT_BYTES = 56 * 1024 * 1024


def _params(*semantics):
    return pltpu.CompilerParams(dimension_semantics=semantics, vmem_limit_bytes=VMEM_LIMIT_BYTES)


def _dot(a, b):
    return jnp.dot(a, b, preferred_element_type=F32)


def _dot_nt(a, b):
    return lax.dot_general(a, b, (((1,), (1,)), ((), ())), preferred_element_type=F32)


def _alibi_slopes(n):
    return jnp.asarray(2.0 ** (-8.0 * np.arange(1, n + 1, dtype=np.float32) / n), F32)


def _rms(x, g):
    return (x * lax.rsqrt(jnp.mean(x * x, axis=-1, keepdims=True) + EPS)) * g


def _norm_mod(x, g, shift, scale):
    return _rms(x, g) * (1 + scale) + shift


def _ada_body(c_ref, w_ref, b_ref, o_ref):
    c = c_ref[...]
    s = (c * jax.nn.sigmoid(c)).astype(BF16)
    o_ref[0] = _dot(s, w_ref[0].astype(BF16)) + b_ref[0]


def _ada(c, w, b, tn):
    nl, d, n = w.shape
    r = c.shape[0]
    return pl.pallas_call(
        _ada_body,
        grid=(nl, n // tn),
        in_specs=[
            pl.BlockSpec((r, d), lambda l, j: (0, 0)),
            pl.BlockSpec((1, d, tn), lambda l, j: (l, 0, j)),
            pl.BlockSpec((1, 1, tn), lambda l, j: (l, 0, j)),
        ],
        out_specs=pl.BlockSpec((1, r, tn), lambda l, j: (l, 0, j)),
        out_shape=jax.ShapeDtypeStruct((nl, r, n), F32),
        compiler_params=_params("arbitrary", "arbitrary"),
        name="ada_mod",
    )(c, w, b.reshape(nl, 1, n))


def _mm_mod_body(bounds, x_ref, g_ref, sh_ref, sc_ref, w_ref, *rest):
    outs, h_ref = rest[:-1], rest[-1]
    j = pl.program_id(1)

    @pl.when(j == 0)
    def _():
        h_ref[...] = _norm_mod(x_ref[...], g_ref[...], sh_ref[0], sc_ref[0]).astype(BF16)

    r = _dot(h_ref[...], w_ref[...].astype(BF16))
    if len(outs) == 1:
        outs[0][...] = r
    else:
        for o_ref, (lo, hi) in zip(outs, bounds):
            @pl.when(jnp.logical_and(j >= lo, j < hi))
            def _(o_ref=o_ref):
                o_ref[...] = r


def _mod_spec(mod, tpb):
    return pl.BlockSpec((1,) + mod.shape[1:], lambda i, j: (i // tpb, 0, 0))


def _mm_mod(x, g, shift, scale, w, layer, splits, tm, tn, tpb):
    m, d = x.shape
    n = w.shape[2]
    bounds, lo = [], 0
    for s in splits:
        bounds.append((lo // tn, (lo + s) // tn))
        lo += s
    out_specs = [
        pl.BlockSpec((tm, tn), lambda i, j, lo=lo_, hi=hi_: (i, jnp.clip(j - lo, 0, hi - lo - 1)))
        for lo_, hi_ in bounds
    ]
    out_shape = [jax.ShapeDtypeStruct((m, s), F32) for s in splits]
    return pl.pallas_call(
        functools.partial(_mm_mod_body, bounds),
        grid=(m // tm, n // tn),
        in_specs=[
            pl.BlockSpec((tm, d), lambda i, j: (i, 0)),
            pl.BlockSpec((1, d), lambda i, j: (0, 0)),
            _mod_spec(shift, tpb),
            _mod_spec(scale, tpb),
            pl.BlockSpec((None, d, tn), lambda i, j: (layer, 0, j)),
        ],
        out_specs=out_specs,
        out_shape=out_shape,
        scratch_shapes=[pltpu.VMEM((tm, d), BF16)],
        compiler_params=_params("arbitrary", "arbitrary"),
        name="norm_mod_matmul",
    )(x, g.reshape(1, d), shift, scale, w)


def _proj_res_body(nj, tn, o_ref, w_ref, x_ref, gate_ref, gn_ref, out_ref, y_ref):
    j = pl.program_id(1)
    y_ref[j] = _dot(o_ref[...], w_ref[...].astype(BF16))

    @pl.when(j == nj - 1)
    def _():
        ss = jnp.zeros((y_ref.shape[1], 1), F32)
        for c in range(nj):
            y = y_ref[c]
            ss = ss + jnp.sum(y * y, axis=-1, keepdims=True)
        rs = lax.rsqrt(ss / (nj * tn) + EPS)
        gate = gate_ref[0]
        for c in range(nj):
            cols = slice(c * tn, (c + 1) * tn)
            out_ref[:, cols] = x_ref[:, cols] + gate[:, cols] * ((y_ref[c] * rs) * gn_ref[:, cols])


def _proj_res(o, w, layer, x, gate, gn, tm, tn, tpb):
    m, k = o.shape
    d = w.shape[2]
    nj = d // tn
    return pl.pallas_call(
        functools.partial(_proj_res_body, nj, tn),
        grid=(m // tm, nj),
        in_specs=[
            pl.BlockSpec((tm, k), lambda i, j: (i, 0)),
            pl.BlockSpec((None, k, tn), lambda i, j: (layer, 0, j)),
            pl.BlockSpec((tm, d), lambda i, j: (i, 0), pipeline_mode=pl.Buffered(1)),
            _mod_spec(gate, tpb),
            pl.BlockSpec((1, d), lambda i, j: (0, 0)),
        ],
        out_specs=pl.BlockSpec((tm, d), lambda i, j: (i, 0)),
        out_shape=jax.ShapeDtypeStruct((m, d), F32),
        scratch_shapes=[pltpu.VMEM((nj, tm, tn), F32)],
        compiler_params=_params("arbitrary", "arbitrary"),
        name="proj_residual",
    )(o, w, x, gate, gn.reshape(1, d))


HALO = 16
HALO_KEEP = 8


def _ffn_body(prompt, nf, tm, tpb, *refs):
    if prompt:
        (x_ref, xh_ref, g_ref, sh_ref, sc_ref, gate_ref, gn_ref, wg_ref, wv_ref, cw_ref, cb_ref, wo_ref,
         prev_ref, out_ref, conv_ref, h_ref, u_ref, act_ref, hh_ref, uh_ref) = refs
    else:
        (x_ref, g_ref, sh_ref, sc_ref, gate_ref, gn_ref, wg_ref, wv_ref, cw_ref, cb_ref, wo_ref,
         prev_ref, out_ref, conv_ref, h_ref, u_ref, act_ref) = refs
    i = pl.program_id(0)
    f = pl.program_id(1)

    @pl.when(f == 0)
    def _():
        h_ref[...] = _norm_mod(x_ref[...], g_ref[...], sh_ref[0], sc_ref[0]).astype(BF16)
        out_ref[...] = jnp.zeros_like(out_ref)
        u_ref[...] = jnp.zeros_like(u_ref)
        act_ref[...] = jnp.zeros_like(act_ref)
        if prompt:
            hh_ref[...] = _norm_mod(xh_ref[...], g_ref[...], sh_ref[0], sc_ref[0]).astype(BF16)
            uh_ref[...] = jnp.zeros_like(uh_ref)

    def down_project():
        out_ref[...] += _dot(act_ref[...], wo_ref[...].astype(BF16))

    def up_project():
        out = []
        for w_ref in (wg_ref, wv_ref):
            w = w_ref[...].astype(BF16)
            nu = _dot(h_ref[...], w)
            nuh = _dot(hh_ref[...], w)[HALO - HALO_KEEP:, :] if prompt else None
            out.append((nu, nuh))
        return out

    def conv_gate():
        row = lax.broadcasted_iota(jnp.int32, (tm, 1), 0)
        acts = []
        for half in range(2):
            u = u_ref[half]
            r1 = pltpu.roll(u, 1, axis=0)
            r2 = pltpu.roll(u, 2, axis=0)
            if prompt:
                uh = uh_ref[half]
                first = (i % tpb) == 0
                m1 = jnp.where(first, prev_ref[0, half, 1:2, :], uh[HALO_KEEP - 1:HALO_KEEP, :])
                m2 = jnp.where(first, prev_ref[0, half, 0:1, :], uh[HALO_KEEP - 2:HALO_KEEP - 1, :])
                u1 = jnp.where(row >= 1, r1, m1)
                u2 = jnp.where(row >= 2, r2, jnp.where(row == 1, m1, m2))
                conv_ref[0, half] = u[tm - 2:tm, :]
            else:
                t = row % 4
                u1 = jnp.where(t >= 1, r1, prev_ref[0, half])
                u2 = jnp.where(t >= 2, r2, prev_ref[1, half])
                conv_ref[half] = u
            cw = cw_ref[half]
            uc = cb_ref[half] + u2 * cw[0:1, :]
            uc = uc + u1 * cw[1:2, :]
            uc = uc + u * cw[2:3, :]
            acts.append(uc)
        live = (f > 0).astype(F32)
        return ((jax.nn.gelu(acts[0], approximate=True) * acts[1]) * live).astype(BF16)

    @pl.when(f < nf)
    def _():
        down_project()
        new = up_project()
        act_ref[...] = conv_gate()
        for half, (nu, nuh) in enumerate(new):
            u_ref[half] = nu
            if prompt:
                uh_ref[half] = nuh

    @pl.when(f >= nf)
    def _():
        down_project()
        act_ref[...] = conv_gate()

    @pl.when(f == nf + 1)
    def _():
        out_ref[...] = x_ref[...] + gate_ref[0] * _rms(out_ref[...], gn_ref[...])


def _ffn(x, g, shift, scale, gate, gn, w_in, conv_w, conv_b, w_out, layer, prev, prompt, tm, tf, tpb):
    m, d = x.shape
    ff = w_out.shape[1]
    nf = ff // tf
    cw = conv_w.reshape(CONV_W, 2, ff).transpose(1, 0, 2)
    cb = conv_b.reshape(2, 1, ff)

    def lag(f, k):
        return jnp.clip(f - k, 0, nf - 1)

    x_spec = pl.BlockSpec((tm, d), lambda i, f: (i, 0), pipeline_mode=pl.Buffered(1))
    out_spec = pl.BlockSpec((tm, d), lambda i, f: (i, 0))
    vec_spec = pl.BlockSpec((1, d), lambda i, f: (0, 0))
    in_specs = [x_spec]
    args = [x]
    if prompt:
        in_specs.append(pl.BlockSpec((HALO, d), lambda i, f: (jnp.maximum(i * (tm // HALO) - 1, 0), 0)))
        args.append(x)
    in_specs += [
        vec_spec, _mod_spec(shift, tpb), _mod_spec(scale, tpb), _mod_spec(gate, tpb), vec_spec,
        pl.BlockSpec((None, d, tf), lambda i, f: (layer, 0, lag(f, 0))),
        pl.BlockSpec((None, d, tf), lambda i, f: (layer, 0, lag(f, 0) + nf)),
        pl.BlockSpec((2, CONV_W, tf), lambda i, f: (0, 0, lag(f, 1))),
        pl.BlockSpec((2, 1, tf), lambda i, f: (0, 0, lag(f, 1))),
        pl.BlockSpec((None, tf, d), lambda i, f: (layer, lag(f, 2), 0)),
    ]
    args += [g.reshape(1, d), shift, scale, gate, gn.reshape(1, d), w_in, w_in, cw, cb, w_out]
    scratch = [pltpu.VMEM((tm, d), BF16), pltpu.VMEM((2, tm, tf), F32), pltpu.VMEM((tm, tf), BF16)]
    if prompt:
        in_specs.append(pl.BlockSpec((1, 2, 2, tf), lambda i, f: (i // tpb, 0, 0, lag(f, 1))))
        conv_spec = pl.BlockSpec((1, 2, 2, tf), lambda i, f: (i, 0, 0, lag(f, 1)))
        conv_shape = jax.ShapeDtypeStruct((m // tm, 2, 2, ff), F32)
        scratch += [pltpu.VMEM((HALO, d), BF16), pltpu.VMEM((2, HALO_KEEP, tf), F32)]
    else:
        in_specs.append(pl.BlockSpec((2, 2, tm, tf), lambda i, f: (0, 0, i, lag(f, 1))))
        conv_spec = pl.BlockSpec((2, tm, tf), lambda i, f: (0, i, lag(f, 1)))
        conv_shape = jax.ShapeDtypeStruct((2, m, ff), F32)
    args.append(prev)
    return pl.pallas_call(
        functools.partial(_ffn_body, prompt, nf, tm, tpb),
        grid=(m // tm, nf + 2),
        in_specs=in_specs,
        out_specs=[out_spec, conv_spec],
        out_shape=[jax.ShapeDtypeStruct((m, d), F32), conv_shape],
        scratch_shapes=scratch,
        compiler_params=_params("arbitrary", "arbitrary"),
        name="conv_ffn_prompt" if prompt else "conv_ffn_sample",
    )(*args)


def _lambda(lp_ref, lam_init):
    lp = lp_ref[...]
    a = jnp.exp(jnp.sum(lp[0:1, :] * lp[1:2, :], axis=-1, keepdims=True))
    b = jnp.exp(jnp.sum(lp[2:3, :] * lp[3:4, :], axis=-1, keepdims=True))
    return a - b + lam_init


def _subln(o, g, lam_init):
    return _rms(o, g) * (1 - lam_init)


def _attn_a_prompt_body(lam_init, tq, qi_ref, ki_ref, slope_ref, q_ref, k_ref, v_ref, lp_ref, sg_ref, o_ref,
                        qs_ref, m_ref, l_ref, acc_ref):
    n = pl.program_id(1)
    step = pl.program_id(2)
    qi = qi_ref[step]
    ki = ki_ref[step]
    scale = A_HEAD_DIM ** -0.5

    @pl.when(ki == 0)
    def _():
        m_ref[...] = jnp.full_like(m_ref, NEG)
        l_ref[...] = jnp.zeros_like(l_ref)
        acc_ref[...] = jnp.zeros_like(acc_ref)
        for c in range(4):
            qs_ref[c] = (q_ref[:, c * A_HEAD_DIM:(c + 1) * A_HEAD_DIM] * scale).astype(BF16)

    def block(masked):
        vb = v_ref[...].astype(BF16)
        kpos = (lax.broadcasted_iota(jnp.int32, (1, tq), 1) + (ki - qi) * tq).astype(F32)
        if masked:
            future = lax.broadcasted_iota(jnp.int32, (tq, tq), 1) > lax.broadcasted_iota(jnp.int32, (tq, tq), 0)
        pairs, scores = [], []
        for i in range(2):
            kb = k_ref[:, i * A_HEAD_DIM:(i + 1) * A_HEAD_DIM].astype(BF16)
            for g in range(2):
                c = g * 2 + i
                s = _dot_nt(qs_ref[c], kb) + slope_ref[n * A_GROUP + g] * kpos
                if masked:
                    s = jnp.where(future, NEG, s)
                pairs.append(c)
                scores.append(s)
        m_prevs = [m_ref[c] for c in pairs]
        m_news = [jnp.maximum(mp, jnp.max(s, axis=-1, keepdims=True)) for mp, s in zip(m_prevs, scores)]
        probs = [jnp.exp(s - mn) for s, mn in zip(scores, m_news)]
        alphas = [jnp.exp(mp - mn) for mp, mn in zip(m_prevs, m_news)]
        for c, p, alpha, mn in zip(pairs, probs, alphas, m_news):
            l_ref[c] = alpha * l_ref[c] + jnp.sum(p, axis=-1, keepdims=True)
            m_ref[c] = mn
        pvs = [_dot(p.astype(BF16), vb) for p in probs]
        for c, alpha, pv in zip(pairs, alphas, pvs):
            acc_ref[c] = alpha * acc_ref[c] + pv

    @pl.when(ki < qi)
    def _():
        block(False)

    @pl.when(ki == qi)
    def _():
        block(True)
        lam = _lambda(lp_ref, lam_init)
        for g in range(2):
            o = acc_ref[g * 2] / l_ref[g * 2] - lam * (acc_ref[g * 2 + 1] / l_ref[g * 2 + 1])
            o_ref[:, g * A_V_DIM:(g + 1) * A_V_DIM] = _subln(o, sg_ref[...], lam_init).astype(BF16)


def _attn_a_prompt(q, k, v, lam_p, subln_g, lam_init, nb, t, tq):
    nq = t // tq
    qi_tab = np.asarray([qi for qi in range(nq) for _ in range(qi + 1)], np.int32)
    ki_tab = np.asarray([ki for qi in range(nq) for ki in range(qi + 1)], np.int32)
    qw = A_GROUP * 2 * A_HEAD_DIM
    grid_spec = pltpu.PrefetchScalarGridSpec(
        num_scalar_prefetch=3,
        grid=(nb, A_KV_HEADS, len(qi_tab)),
        in_specs=[
            pl.BlockSpec((tq, qw), lambda b, n, s, qi, ki, sl: (b * nq + qi[s], n)),
            pl.BlockSpec((tq, 2 * A_HEAD_DIM), lambda b, n, s, qi, ki, sl: (b * nq + ki[s], n)),
            pl.BlockSpec((tq, A_V_DIM), lambda b, n, s, qi, ki, sl: (b * nq + ki[s], n)),
            pl.BlockSpec((4, A_HEAD_DIM), lambda b, n, s, qi, ki, sl: (0, 0)),
            pl.BlockSpec((1, A_V_DIM), lambda b, n, s, qi, ki, sl: (0, 0)),
        ],
        out_specs=pl.BlockSpec((tq, A_GROUP * A_V_DIM), lambda b, n, s, qi, ki, sl: (b * nq + qi[s], n)),
        scratch_shapes=[
            pltpu.VMEM((4, tq, A_HEAD_DIM), BF16),
            pltpu.VMEM((4, tq, 1), F32),
            pltpu.VMEM((4, tq, 1), F32),
            pltpu.VMEM((4, tq, A_V_DIM), F32),
        ],
    )
    return pl.pallas_call(
        functools.partial(_attn_a_prompt_body, lam_init, tq),
        grid_spec=grid_spec,
        out_shape=jax.ShapeDtypeStruct((nb * t, A_KV_HEADS * A_GROUP * A_V_DIM), BF16),
        compiler_params=_params("arbitrary", "arbitrary", "arbitrary"),
        name="diff_attn_prompt",
    )(jnp.asarray(qi_tab), jnp.asarray(ki_tab), _alibi_slopes(A_KV_HEADS * A_GROUP), q, k, v, lam_p,
      subln_g.reshape(1, A_V_DIM))


PAGES_PER_STEP = 16
ROWS_PER_PAGE = PAGE * 8


def _attn_a_decode_body(lam_init, past_len, nchunk, pt_ref, slope_ref, q_ref, kn_ref, vn_ref, lp_ref, sg_ref, *rest):
    npg = PAGES_PER_STEP
    k_refs, v_refs = rest[:npg], rest[npg:2 * npg]
    o_ref, m_ref, l_ref, acc_ref = rest[2 * npg:]
    c = pl.program_id(1)
    scale = A_HEAD_DIM ** -0.5
    nrow = 2 * A_GROUP * 4
    g_row = (lax.broadcasted_iota(jnp.int32, (nrow, 1), 0) // 4) % 2
    t_row = lax.broadcasted_iota(jnp.int32, (nrow, 1), 0) % 4

    @pl.when(c == 0)
    def _():
        m_ref[...] = jnp.full_like(m_ref, NEG)
        l_ref[...] = jnp.zeros_like(l_ref)
        acc_ref[...] = jnp.zeros_like(acc_ref)

    def slope_col(n):
        return jnp.where(g_row == 0, slope_ref[n * A_GROUP], slope_ref[n * A_GROUP + 1])

    kpos = (lax.broadcasted_iota(jnp.int32, (1, npg * PAGE), 1) + (c * (npg * PAGE) - past_len)).astype(F32)
    for n in range(A_KV_HEADS):
        s_rows = []
        for i in range(2):
            qni = (q_ref[0, n * 2 + i] * scale).astype(BF16)
            parts = []
            for r in range(npg):
                kp = k_refs[r][pl.ds(n * 2 + i, PAGE, stride=8), :].astype(BF16)
                parts.append(_dot_nt(qni, kp))
            s_rows.append(jnp.concatenate(parts, axis=1))
        s = jnp.concatenate(s_rows, axis=0) + slope_col(n) * kpos
        m_prev = m_ref[n]
        m_new = jnp.maximum(m_prev, jnp.max(s, axis=-1, keepdims=True))
        alpha = jnp.exp(m_prev - m_new)
        p = jnp.exp(s - m_new)
        l_ref[n] = alpha * l_ref[n] + jnp.sum(p, axis=-1, keepdims=True)
        pb = p.astype(BF16)
        pv = jnp.zeros((nrow, A_V_DIM), F32)
        for r in range(npg):
            vp = jnp.concatenate([v_refs[r][pl.ds(n, PAGE, stride=8), :],
                                  v_refs[r][pl.ds(4 + n, PAGE, stride=8), :]], axis=1).astype(BF16)
            pv = pv + _dot(pb[:, r * PAGE:(r + 1) * PAGE], vp)
        acc_ref[n] = alpha * acc_ref[n] + pv
        m_ref[n] = m_new

    @pl.when(c == nchunk - 1)
    def _():
        lam = _lambda(lp_ref, lam_init)
        for n in range(A_KV_HEADS):
            sl = slope_col(n)
            cols = []
            for tk in range(4):
                parts = []
                for i in range(2):
                    qni = q_ref[0, n * 2 + i] * scale
                    parts.append(jnp.sum(qni * kn_ref[0, n * 2 + i, tk:tk + 1, :], axis=-1, keepdims=True))
                s_tk = jnp.concatenate(parts, axis=0) + sl * float(tk)
                cols.append(jnp.where(t_row >= tk, s_tk, NEG))
            m_prev = m_ref[n]
            m_new = m_prev
            for s_tk in cols:
                m_new = jnp.maximum(m_new, s_tk)
            alpha = jnp.exp(m_prev - m_new)
            l_new = alpha * l_ref[n]
            acc = alpha * acc_ref[n]
            for tk, s_tk in enumerate(cols):
                p_tk = jnp.exp(s_tk - m_new)
                l_new = l_new + p_tk
                acc = acc + p_tk * vn_ref[0, n, tk:tk + 1, :]
            out = acc / l_new
            half = A_GROUP * 4
            o = out[:half] - lam * out[half:]
            o_ref[0, n] = _subln(o, sg_ref[...], lam_init)


def _attn_a_decode(q, kn, vn, ck_rows, cv_rows, page_table, layer, n_pool, lam_p, subln_g, lam_init, past_len):
    nb = q.shape[0]
    n_pages = page_table.shape[1]
    npg = PAGES_PER_STEP
    nchunk = n_pages // npg
    base = layer * n_pool

    def page_spec(r):
        return pl.BlockSpec((ROWS_PER_PAGE, 128), lambda b, c, pt, sl: (pt[b * n_pages + c * npg + r] + base, 0))

    grid_spec = pltpu.PrefetchScalarGridSpec(
        num_scalar_prefetch=2,
        grid=(nb, nchunk),
        in_specs=[
            pl.BlockSpec((1, 8, 8, A_HEAD_DIM), lambda b, c, pt, sl: (b, 0, 0, 0)),
            pl.BlockSpec((1, 8, 4, A_HEAD_DIM), lambda b, c, pt, sl: (b, 0, 0, 0)),
            pl.BlockSpec((1, 4, 4, A_V_DIM), lambda b, c, pt, sl: (b, 0, 0, 0)),
            pl.BlockSpec((4, A_HEAD_DIM), lambda b, c, pt, sl: (0, 0)),
            pl.BlockSpec((1, A_V_DIM), lambda b, c, pt, sl: (0, 0)),
        ] + [page_spec(r) for r in range(npg)] + [page_spec(r) for r in range(npg)],
        out_specs=pl.BlockSpec((1, A_KV_HEADS, A_GROUP * 4, A_V_DIM), lambda b, c, pt, sl: (b, 0, 0, 0)),
        scratch_shapes=[
            pltpu.VMEM((A_KV_HEADS, 16, 1), F32),
            pltpu.VMEM((A_KV_HEADS, 16, 1), F32),
            pltpu.VMEM((A_KV_HEADS, 16, A_V_DIM), F32),
        ],
    )
    return pl.pallas_call(
        functools.partial(_attn_a_decode_body, lam_init, past_len, nchunk),
        grid_spec=grid_spec,
        out_shape=jax.ShapeDtypeStruct((nb, A_KV_HEADS, A_GROUP * 4, A_V_DIM), F32),
        compiler_params=_params("arbitrary", "arbitrary"),
        name="diff_attn_decode",
    )(page_table.reshape(-1), _alibi_slopes(A_KV_HEADS * A_GROUP), q, kn, vn, lam_p, subln_g.reshape(1, A_V_DIM),
      *([ck_rows] * npg), *([cv_rows] * npg))


def _attn_b_body(rows, ntok, nseq, first_lo_invalid, nkb, slope_ref, sink_ref, q_ref, *rest):
    k_refs, v_refs, o_ref = rest[:nkb], rest[nkb:2 * nkb], rest[2 * nkb]
    j = pl.program_id(1)
    scale = B_HEAD_DIM ** -0.5
    nr = B_GROUP * rows
    nk = nseq * 2 * WINDOW
    r = lax.broadcasted_iota(jnp.int32, (nr, nk), 0) % rows
    c = lax.broadcasted_iota(jnp.int32, (nr, nk), 1)
    kk = c % (2 * WINDOW)
    dist = r % ntok - kk + WINDOW
    valid = jnp.logical_and(dist >= 0, dist < WINDOW)
    if nseq > 1:
        valid = jnp.logical_and(valid, r // ntok == c // (2 * WINDOW))
    if first_lo_invalid:
        valid = jnp.logical_and(valid, jnp.logical_or(kk >= WINDOW, j > 0))
    distf = dist.astype(F32)
    g_col = lax.broadcasted_iota(jnp.int32, (nr, 1), 0) // rows

    def per_row(ref, n):
        out = jnp.zeros((nr, 1), F32)
        for g in range(B_GROUP):
            out = jnp.where(g_col == g, ref[n * B_GROUP + g], out)
        return out

    for n in range(B_KV_HEADS):
        ksl = slice(n * B_HEAD_DIM, (n + 1) * B_HEAD_DIM)
        kw = jnp.concatenate([kr[0, :, ksl] for kr in k_refs], axis=0).astype(BF16)
        vw = jnp.concatenate([vr[0, :, ksl] for vr in v_refs], axis=0).astype(BF16)
        q4 = jnp.concatenate([q_ref[0, :, (n * B_GROUP + g) * B_HEAD_DIM:(n * B_GROUP + g + 1) * B_HEAD_DIM]
                              for g in range(B_GROUP)], axis=0)
        s = jnp.where(valid, _dot_nt((q4 * scale).astype(BF16), kw) - per_row(slope_ref, n) * distf, NEG)
        sink = per_row(sink_ref, n)
        m = jnp.maximum(jnp.max(s, axis=-1, keepdims=True), sink)
        e = jnp.exp(s - m)
        denom = jnp.sum(e, axis=-1, keepdims=True) + jnp.exp(sink - m)
        p = (e * (1.0 / denom)).astype(BF16)
        o = _dot(p, vw)
        for g in range(B_GROUP):
            h = n * B_GROUP + g
            o_ref[0, :, h * B_HEAD_DIM:(h + 1) * B_HEAD_DIM] = o[g * rows:(g + 1) * rows].astype(BF16)


def _attn_b(q, k, v, sinks, ntok, nseq, nblk, key_maps, first_lo_invalid):
    nbat = q.shape[0] // nblk
    rows, d = q.shape[1], q.shape[2]
    kr, kd = k.shape[1], k.shape[2]
    nkb = len(key_maps)
    kv_specs = [pl.BlockSpec((1, kr, kd), lambda b, j, sl, sk, km=km: (km(b, j), 0, 0)) for km in key_maps]
    grid_spec = pltpu.PrefetchScalarGridSpec(
        num_scalar_prefetch=2,
        grid=(nbat, nblk),
        in_specs=[pl.BlockSpec((1, rows, d), lambda b, j, sl, sk: (b * nblk + j, 0, 0))] + kv_specs + kv_specs,
        out_specs=pl.BlockSpec((1, rows, d), lambda b, j, sl, sk: (b * nblk + j, 0, 0)),
    )
    return pl.pallas_call(
        functools.partial(_attn_b_body, rows, ntok, nseq, first_lo_invalid, nkb),
        grid_spec=grid_spec,
        out_shape=jax.ShapeDtypeStruct(q.shape, BF16),
        compiler_params=_params("arbitrary", "arbitrary"),
        name="swa_attn",
    )(_alibi_slopes(B_KV_HEADS * B_GROUP), sinks.astype(F32), q, *([k] * nkb), *([v] * nkb))


SWA_SEQS_PER_TILE = 8


def _lam_init(layer):
    return 0.8 - 0.6 * math.exp(-0.3 * layer)


def _trunk(x, mods, kv_mod, prompt, nb, t, w, cache=None, state_b=None, state_conv=None):
    m, d = x.shape
    depth = len(mods)
    n_a = depth // 2
    ff = w["w_ffn_out"].shape[1]
    if prompt:
        tm_mm, tm, tq_a, tf = 1024, 1024, 512, 256
    else:
        tm_mm, tm, tf = m, m, 256
    tpb_mm, tpb = max(t // tm_mm, 1), max(t // tm, 1)
    kd = B_KV_HEADS * B_HEAD_DIM
    a_k, a_v, convs = [], [], []
    kb = vb = None
    for l in range(depth):
        sh1, sc1, g1, sh2, sc2, g2 = mods[l]
        ng = w["norm_g"][l]
        if l < n_a:
            q, k, v = _mm_mod(x, ng[0], sh1, sc1, w["w_qkv_a"], l, (2048, 1024, 1024), tm_mm, 512, tpb_mm)
            a_k.append(k)
            a_v.append(v)
            lam_init = _lam_init(l)
            if prompt:
                o = _attn_a_prompt(q, k, v, w["lambda_a"][l], w["subln_g"][l], lam_init, nb, t, tq_a)
            else:
                ck_rows, cv_rows, page_table, n_pool, past_len = cache
                qd = q.reshape(nb, t, A_KV_HEADS, A_GROUP, 2, A_HEAD_DIM).transpose(0, 2, 4, 3, 1, 5)
                qd = qd.reshape(nb, A_KV_HEADS * 2, A_GROUP * t, A_HEAD_DIM)
                kn = k.reshape(nb, t, A_KV_HEADS * 2, A_HEAD_DIM).transpose(0, 2, 1, 3)
                vn = v.reshape(nb, t, A_KV_HEADS, A_V_DIM).transpose(0, 2, 1, 3)
                od = _attn_a_decode(qd, kn, vn, ck_rows, cv_rows, page_table, l, n_pool, w["lambda_a"][l],
                                    w["subln_g"][l], lam_init, past_len)
                o = od.reshape(nb, A_KV_HEADS, A_GROUP, t, A_V_DIM).transpose(0, 3, 1, 2, 4).reshape(m, d).astype(BF16)
            x = _proj_res(o, w["w_o_a"], l, x, g1, ng[1], tm, 512, tpb)
        else:
            jb = l - n_a
            (q,) = _mm_mod(x, ng[0], sh1, sc1, w["w_q_b"], jb, (d,), tm_mm, 512, tpb_mm)
            if prompt:
                nblk = t // WINDOW
                o = _attn_b(q.reshape(nb * nblk, WINDOW, d), kb.reshape(nb * nblk, WINDOW, kd),
                            vb.reshape(nb * nblk, WINDOW, kd), w["sinks_b"][jb], WINDOW, 1, nblk,
                            (lambda b, j: jnp.maximum(b * nblk + j - 1, 0), lambda b, j: b * nblk + j), True)
            else:
                kcat, vcat = state_b
                ns = SWA_SEQS_PER_TILE
                o = _attn_b(q.reshape(nb // ns, ns * t, d), kcat.reshape(nb // ns, ns * 2 * WINDOW, kd),
                            vcat.reshape(nb // ns, ns * 2 * WINDOW, kd), w["sinks_b"][jb], t, ns, 1,
                            (lambda b, j: b,), False)
            x = _proj_res(o.reshape(m, d), w["w_o_b"], jb, x, g1, ng[1], tm, 512, tpb)
        if prompt:
            prev = jnp.zeros((nb, 2, 2, ff), F32)
        else:
            sc = state_conv[l].reshape(nb, 2, 2, ff)
            zero = jnp.zeros((nb, 2, ff), F32)
            p1 = jnp.stack([sc[:, 1], zero, zero, zero], axis=1)
            p2 = jnp.stack([sc[:, 0], sc[:, 1], zero, zero], axis=1)
            prev = jnp.stack([p1, p2]).reshape(2, m, 2, ff).transpose(0, 2, 1, 3)
        x, conv = _ffn(x, ng[2], sh2, sc2, g2, ng[3], w["w_ffn_in"], w["conv_w"][l], w["conv_b"][l],
                       w["w_ffn_out"], l, prev, prompt, tm, tf, tpb)
        if prompt:
            convs.append(conv[tpb - 1::tpb].transpose(0, 2, 1, 3).reshape(nb, 2, 2 * ff))
        else:
            u = conv.reshape(2, nb, t, ff)[:, :, t - 2:, :]
            convs.append(u.transpose(1, 2, 0, 3).reshape(nb, 2, 2 * ff))
        if l == n_a - 1:
            kv_shift, kv_scale = kv_mod
            kb, vb = _mm_mod(x, w["kv_norm_g"], kv_shift, kv_scale, w["w_kv_b"][None], 0, (kd, kd), tm_mm, 512, tpb_mm)
            if prompt:
                new_b = (kb.reshape(nb, t, kd)[:, -WINDOW:], vb.reshape(nb, t, kd)[:, -WINDOW:])
            else:
                sk, sv = state_b
                pad = jnp.zeros((nb, WINDOW - t, kd), F32)
                kcat = jnp.concatenate([sk.reshape(nb, WINDOW, kd), kb.reshape(nb, t, kd), pad], axis=1)
                vcat = jnp.concatenate([sv.reshape(nb, WINDOW, kd), vb.reshape(nb, t, kd), pad], axis=1)
                state_b = (kcat, vcat)
                new_b = (kcat[:, t:t + WINDOW], vcat[:, t:t + WINDOW])
    return x, a_k, a_v, new_b, convs


def kernel(x_prompt, x_sample, c_prompt, c_sample, cache_a_k, cache_a_v, page_table, state_b_k, state_b_v, state_conv, ada_w, ada_b, norm_g, w_qkv_a, lambda_a, subln_g, w_o_a, kv_norm_g, kv_ada_w, kv_ada_b, w_kv_b, w_q_b, sinks_b, w_o_b, w_ffn_in, conv_w, conv_b, w_ffn_out):
    bp, tp, d = x_prompt.shape
    bs, ts, _ = x_sample.shape
    depth = ada_w.shape[0]
    n_a, n_pool = cache_a_k.shape[0], cache_a_k.shape[1]
    past_len = page_table.shape[1] * PAGE
    w = dict(norm_g=norm_g, w_qkv_a=w_qkv_a, lambda_a=lambda_a, subln_g=subln_g, w_o_a=w_o_a, kv_norm_g=kv_norm_g,
             w_kv_b=w_kv_b, w_q_b=w_q_b, sinks_b=sinks_b, w_o_b=w_o_b, w_ffn_in=w_ffn_in, conv_w=conv_w,
             conv_b=conv_b, w_ffn_out=w_ffn_out)

    nrow = bp + bs
    rpad = -nrow % 8
    c_all = jnp.concatenate([c_prompt, c_sample, jnp.zeros((rpad, d), F32)], axis=0)
    mod = _ada(c_all, ada_w, ada_b, 1024)
    kvm = _ada(c_all, kv_ada_w[None], kv_ada_b[None], 1024)[0]

    def prompt_vec(a):
        return a.reshape(bp, 1, d)

    def sample_vec(a):
        return jnp.repeat(a, ts, axis=0).reshape(1, bs * ts, d)

    def split_mods(rows, vec):
        return [[vec(mod[l, rows, i * d:(i + 1) * d]) for i in range(6)] for l in range(depth)]

    p_rows, s_rows = slice(0, bp), slice(bp, bp + bs)
    mods_p, mods_s = split_mods(p_rows, prompt_vec), split_mods(s_rows, sample_vec)
    kv_p = [prompt_vec(kvm[p_rows, i * d:(i + 1) * d]) for i in range(2)]
    kv_s = [sample_vec(kvm[s_rows, i * d:(i + 1) * d]) for i in range(2)]

    yp, akp, avp, nbp, cvp = _trunk(x_prompt.reshape(bp * tp, d), mods_p, kv_p, True, bp, tp, w)

    rows = n_a * n_pool * ROWS_PER_PAGE
    ck_rows = cache_a_k.reshape(rows, 128)
    cv_rows = cache_a_v.reshape(n_a, n_pool, PAGE, A_KV_HEADS, 2, 128).transpose(0, 1, 2, 4, 3, 5).reshape(rows, 128)
    ys, aks, avs, nbs, cvs = _trunk(x_sample.reshape(bs * ts, d), mods_s, kv_s, False, bs, ts, w,
                                    cache=(ck_rows, cv_rows, page_table, n_pool, past_len),
                                    state_b=(state_b_k, state_b_v), state_conv=state_conv)

    def stack_k(ks, b, t):
        return jnp.stack(ks).reshape(n_a, b, t, A_KV_HEADS, 2, A_HEAD_DIM)

    def stack_v(vs, b, t):
        return jnp.stack(vs).reshape(n_a, b, t, A_KV_HEADS, A_V_DIM)

    def bshape(a, b):
        return a.reshape(b, WINDOW, B_KV_HEADS, B_HEAD_DIM)

    return (yp.reshape(bp, tp, d), ys.reshape(bs, ts, d),
            stack_k(akp, bp, tp), stack_v(avp, bp, tp), stack_k(aks, bs, ts), stack_v(avs, bs, ts),
            bshape(nbp[0], bp), bshape(nbp[1], bp), bshape(nbs[0], bs), bshape(nbs[1], bs),
            jnp.stack(cvp), jnp.stack(cvs))
```
